```python
import math
import jax, jax.numpy as jnp
from jax import lax
import numpy as np

D_MODEL = 1024
BATCH = 8
SEQ = 4096
DEPTH = 1

GDN_HEADS = 8
GDN_HEAD_DIM = 64
GDN_WIDTH = GDN_HEADS * GDN_HEAD_DIM
GDN_CONV = 4
GDN_CHUNK = 64
RWKV_HEADS = 8
RWKV_HEAD_DIM = 64
RWKV_WIDTH = RWKV_HEADS * RWKV_HEAD_DIM
DECAY_LORA = 64
ICLR_LORA = 64
GATE_LORA = 160
RWKV_COLS = 3 * RWKV_WIDTH + DECAY_LORA + ICLR_LORA + GATE_LORA
IN_SIZES = (3 * GDN_WIDTH, GDN_WIDTH, GDN_HEADS, GDN_HEADS, RWKV_COLS, D_MODEL, D_MODEL)
D_IN = 3 * GDN_WIDTH + GDN_WIDTH + 2 * GDN_HEADS + RWKV_COLS + 2 * D_MODEL
D_FF = 2816
FFN_CONV = 3
NORM_EPS = 1e-6
LNX_EPS = 64e-5

kernel_name = "hybrid_gdn_rwkv7_convglu_adaln"


def rms_norm(x, w, eps=NORM_EPS):
    xf = x.astype(jnp.float32)
    y = xf * lax.rsqrt(jnp.mean(xf * xf, axis=-1, keepdims=True) + eps)
    return (y * w.astype(jnp.float32)).astype(x.dtype)


def l2_normalize(x, eps=1e-6):
    xf = x.astype(jnp.float32)
    return xf * lax.rsqrt(jnp.sum(xf * xf, axis=-1, keepdims=True) + eps)


def split_cols(p, sizes):
    out, start = [], 0
    for s in sizes:
        out.append(p[..., start:start + s])
        start += s
    return out


def token_shift(p):
    return jnp.pad(p, ((0, 0), (1, 0), (0, 0)))[:, :-1]


def causal_depthwise_conv(x, w):
    k = w.shape[0]
    return lax.conv_general_dilated(
        x, w[:, None, :].astype(x.dtype), window_strides=(1,), padding=((k - 1, 0),),
        dimension_numbers=('NWC', 'WIO', 'NWC'), feature_group_count=x.shape[-1])


def chunk_gated_delta_rule(q, k, v, g, beta):
    B, T, H, dk = q.shape
    dv = v.shape[-1]
    C = GDN_CHUNK
    n = T // C
    to_chunks = lambda t: t.reshape(B, n, C, H, -1).transpose(0, 3, 1, 2, 4)
    q, k, v = to_chunks(q), to_chunks(k), to_chunks(v)
    g = g.reshape(B, n, C, H).transpose(0, 3, 1, 2)
    beta = beta.reshape(B, n, C, H).transpose(0, 3, 1, 2)
    g_cum = jnp.cumsum(g, axis=-1)
    causal = jnp.tril(jnp.ones((C, C), dtype=bool))
    strict = jnp.tril(jnp.ones((C, C), dtype=bool), -1)
    diff = g_cum[..., :, None] - g_cum[..., None, :]
    decay = jnp.where(causal, jnp.exp(jnp.where(causal, diff, 0.0)), 0.0)
    k_beta = k * beta[..., None]
    v_beta = v * beta[..., None]
    lower = jnp.where(strict, jnp.einsum('bhncd,bhnsd->bhncs', k_beta, k) * decay, 0.0)
    eye = jnp.eye(C, dtype=q.dtype)
    t_mat = lax.linalg.triangular_solve(lower + eye, jnp.broadcast_to(eye, lower.shape),
                                        left_side=True, lower=True, unit_diagonal=True)
    u = jnp.matmul(t_mat, v_beta)
    w = jnp.matmul(t_mat, k_beta * jnp.exp(g_cum)[..., None])
    attn = jnp.where(causal, jnp.einsum('bhncd,bhnsd->bhncs', q, k) * decay, 0.0)

    def step(S, inp):
        q_i, k_i, u_i, w_i, a_i, gc_i = inp
        v_new = u_i - jnp.matmul(w_i, S)
        o_i = jnp.matmul(q_i * jnp.exp(gc_i)[..., None], S) + jnp.matmul(a_i, v_new)
        g_last = gc_i[..., -1]
        k_dec = k_i * jnp.exp(g_last[..., None] - gc_i)[..., None]
        S = S * jnp.exp(g_last)[..., None, None] + jnp.einsum('bhcd,bhce->bhde', k_dec, v_new)
        return S, o_i

    mv = lambda t: jnp.moveaxis(t, 2, 0)
    S0 = jnp.zeros((B, H, dk, dv), dtype=q.dtype)
    _, o = lax.scan(step, S0, (mv(q), mv(k), mv(u), mv(w), mv(attn), mv(g_cum)))
    return o.transpose(1, 0, 3, 2, 4).reshape(B, T, H, dv)


def gated_deltanet(qkv, z, b_logit, a_logit, conv_w, a_log, dt_bias, onorm_w):
    B, T, _ = qkv.shape
    qkv = jax.nn.silu(causal_depthwise_conv(qkv, conv_w)).astype(jnp.float32)
    q, k, v = jnp.split(qkv, 3, axis=-1)
    heads = lambda t: t.reshape(B, T, GDN_HEADS, GDN_HEAD_DIM)
    q = l2_normalize(heads(q)) * (GDN_HEAD_DIM ** -0.5)
    k = l2_normalize(heads(k))
    v = heads(v)
    beta = jax.nn.sigmoid(b_logit.astype(jnp.float32))
    g = -jnp.exp(a_log.astype(jnp.float32)) * jax.nn.softplus(
        a_logit.astype(jnp.float32) + dt_bias.astype(jnp.float32))
    o = chunk_gated_delta_rule(q, k, v, g, beta)
    o = rms_norm(o, onorm_w) * jax.nn.silu(heads(z).astype(jnp.float32))
    return o.reshape(B, T, GDN_WIDTH)


def rwkv7_scan(r, decay, k, v, a, b):
    B, T, H, N = r.shape

    def step(S, inp):
        r_t, w_t, k_t, v_t, a_t, b_t = inp
        sa = jnp.einsum('bhvk,bhk->bhv', S, a_t)
        S = S * w_t[:, :, None, :] + sa[..., None] * b_t[:, :, None, :] + v_t[..., None] * k_t[:, :, None, :]
        return S, jnp.einsum('bhvk,bhk->bhv', S, r_t)

    tm = lambda t: jnp.moveaxis(t, 1, 0)
    S0 = jnp.zeros((B, H, N, N), dtype=jnp.float32)
    _, y = lax.scan(step, S0, (tm(r), tm(decay), tm(k), tm(v), tm(a), tm(b)))
    return jnp.moveaxis(y, 0, 1)


def rwkv7_time_mix(pb, mu, w0, w2, a0, a2, g2, k_k, k_a, r_k, lnx_w, lnx_b):
    f32 = lambda t: t.astype(jnp.float32)
    B, T, _ = pb.shape
    pb = f32(pb)
    pb = pb + (token_shift(pb) - pb) * f32(mu)
    r, k, v, w_lo, a_lo, g_lo = split_cols(
        pb, (RWKV_WIDTH, RWKV_WIDTH, RWKV_WIDTH, DECAY_LORA, ICLR_LORA, GATE_LORA))
    w = -jax.nn.softplus(-(f32(w0) + jnp.tanh(w_lo) @ f32(w2))) - 0.5
    decay = jnp.exp(-jnp.exp(w))
    a = jax.nn.sigmoid(f32(a0) + a_lo @ f32(a2))
    g = jax.nn.sigmoid(g_lo) @ f32(g2)
    heads = lambda t: t.reshape(B, T, RWKV_HEADS, RWKV_HEAD_DIM)
    kk = l2_normalize(heads(k * f32(k_k)))
    k = k * (1.0 + (a - 1.0) * f32(k_a))
    r, k, v, a, decay = heads(r), heads(k), heads(v), heads(a), heads(decay)
    y = rwkv7_scan(r, decay, k, v, -kk, kk * a)
    mean = jnp.mean(y, axis=-1, keepdims=True)
    var = jnp.mean(jnp.square(y - mean), axis=-1, keepdims=True)
    y = ((y - mean) * lax.rsqrt(var + LNX_EPS)).reshape(B, T, RWKV_WIDTH) * f32(lnx_w) + f32(lnx_b)
    bonus = jnp.sum(r * k * f32(r_k), axis=-1, keepdims=True) * v
    return (y + bonus.reshape(B, T, RWKV_WIDTH)) * g


def conv_glu(h, w_in, conv_w, w_out):
    gate, up = jnp.split(h @ w_in, 2, axis=-1)
    gate = causal_depthwise_conv(gate, conv_w)
    return (jax.nn.silu(gate) * up) @ w_out


def setup_inputs(seed: int = 0) -> dict:
    key = jax.random.key(seed)
    ks = iter(jax.random.split(key, 40))
    nrm = lambda shape, scale: jax.random.normal(next(ks), shape, jnp.float32) * scale
    uni = lambda shape, lo, hi: jax.random.uniform(next(ks), shape, jnp.float32, minval=lo, maxval=hi)
    L = DEPTH
    x = nrm((BATCH, SEQ, D_MODEL), 1.0)
    c = nrm((BATCH, D_MODEL), 1.0)
    w_ada = nrm((L, D_MODEL, 6 * D_MODEL), D_MODEL ** -0.5)
    b_ada = nrm((L, 6 * D_MODEL), 0.02)
    norm1_w = 1.0 + nrm((L, D_MODEL), 0.02)
    w_in = nrm((L, D_MODEL, D_IN), D_MODEL ** -0.5)
    conv_gdn = nrm((L, GDN_CONV, 3 * GDN_WIDTH), GDN_CONV ** -0.5)
    a_log = jnp.log(uni((L, GDN_HEADS), 1.0, 16.0))
    dt = jnp.exp(uni((L, GDN_HEADS), math.log(1e-3), math.log(1e-1)))
    dt_bias = dt + jnp.log(-jnp.expm1(-dt))
    onorm_gdn = 1.0 + nrm((L, GDN_HEAD_DIM), 0.02)
    w_branch_gdn = nrm((L, GDN_WIDTH, D_MODEL), GDN_WIDTH ** -0.5)
    mu_rwkv = uni((L, RWKV_COLS), 0.0, 1.0)
    w0 = uni((L, RWKV_WIDTH), -6.5, -1.5)
    w2 = nrm((L, DECAY_LORA, RWKV_WIDTH), 0.5 * DECAY_LORA ** -0.5)
    a0 = nrm((L, RWKV_WIDTH), 0.1)
    a2 = nrm((L, ICLR_LORA, RWKV_WIDTH), 0.5 * ICLR_LORA ** -0.5)
    g2 = nrm((L, GATE_LORA, RWKV_WIDTH), GATE_LORA ** -0.5)
    k_k = 0.85 + nrm((L, RWKV_WIDTH), 0.02)
    k_a = 1.0 + nrm((L, RWKV_WIDTH), 0.02)
    r_k = nrm((L, RWKV_HEADS, RWKV_HEAD_DIM), 0.1)
    lnx_w = 1.0 + nrm((L, RWKV_WIDTH), 0.02)
    lnx_b = nrm((L, RWKV_WIDTH), 0.02)
    w_branch_rwkv = nrm((L, RWKV_WIDTH, D_MODEL), RWKV_WIDTH ** -0.5)
    w_out = nrm((L, D_MODEL, D_MODEL), D_MODEL ** -0.5)
    norm2_w = 1.0 + nrm((L, D_MODEL), 0.02)
    w_ffn_in = nrm((L, D_MODEL, 2 * D_FF), D_MODEL ** -0.5)
    conv_ffn = nrm((L, FFN_CONV, D_FF), FFN_CONV ** -0.5)
    w_ffn_out = nrm((L, D_FF, D_MODEL), D_FF ** -0.5)
    norm_f_w = 1.0 + nrm((D_MODEL,), 0.02)
    return {"x": x, "c": c, "w_ada": w_ada, "b_ada": b_ada, "norm1_w": norm1_w, "w_in": w_in,
            "conv_gdn": conv_gdn, "a_log": a_log, "dt_bias": dt_bias, "onorm_gdn": onorm_gdn,
            "w_branch_gdn": w_branch_gdn, "mu_rwkv": mu_rwkv, "w0": w0, "w2": w2, "a0": a0, "a2": a2,
            "g2": g2, "k_k": k_k, "k_a": k_a, "r_k": r_k, "lnx_w": lnx_w, "lnx_b": lnx_b,
            "w_branch_rwkv": w_branch_rwkv, "w_out": w_out, "norm2_w": norm2_w, "w_ffn_in": w_ffn_in,
            "conv_ffn": conv_ffn, "w_ffn_out": w_ffn_out, "norm_f_w": norm_f_w}


def reference(x, c, w_ada, b_ada, norm1_w, w_in, conv_gdn, a_log, dt_bias, onorm_gdn, w_branch_gdn,
              mu_rwkv, w0, w2, a0, a2, g2, k_k, k_a, r_k, lnx_w, lnx_b, w_branch_rwkv, w_out,
              norm2_w, w_ffn_in, conv_ffn, w_ffn_out, norm_f_w):
    cond = jax.nn.silu(c)
    for i in range(DEPTH):
        mod = cond @ w_ada[i] + b_ada[i]
        shift1, scale1, gate1, shift2, scale2, gate2 = jnp.split(mod[:, None, :], 6, axis=-1)
        h = rms_norm(x, norm1_w[i]) * (1.0 + scale1) + shift1
        p = h @ w_in[i]
        qkv_a, z_a, b_a, a_a, p_b, gl_a, gl_b = split_cols(p, IN_SIZES)
        y_a = gated_deltanet(qkv_a, z_a, b_a, a_a, conv_gdn[i], a_log[i], dt_bias[i], onorm_gdn[i])
        y_b = rwkv7_time_mix(p_b, mu_rwkv[i], w0[i], w2[i], a0[i], a2[i], g2[i], k_k[i], k_a[i],
                             r_k[i], lnx_w[i], lnx_b[i])
        y_a = y_a.astype(x.dtype) @ w_branch_gdn[i]
        y_b = y_b.astype(x.dtype) @ w_branch_rwkv[i]
        merged = jax.nn.sigmoid(gl_a) * y_a + jax.nn.sigmoid(gl_b) * y_b
        x = x + gate1 * (merged @ w_out[i])
        h = rms_norm(x, norm2_w[i]) * (1.0 + scale2) + shift2
        x = x + gate2 * conv_glu(h, w_ffn_in[i], conv_ffn[i], w_ffn_out[i])
    return rms_norm(x, norm_f_w)
```

```python
import functools

import jax
import jax.numpy as jnp
from jax import lax
from jax.experimental import pallas as pl
from jax.experimental.pallas import tpu as pltpu

F32 = jnp.float32
BF16 = jnp.bfloat16

D_MODEL = 1024
HEADS = 8
HEAD_DIM = 64
HEAD_SHIFT = 6
WIDTH = HEADS * HEAD_DIM
PAIRS = HEADS // 2
LANES = 128
SUBLANES = 8
CHUNK = 64
GDN_CONV = 4
DECAY_LORA = 64
ICLR_LORA = 64
GATE_LORA = 160
GATE_LORA_PAD = 256
D_FF = 2816
FFN_CONV = 3
NORM_EPS = 1e-6
LNX_EPS = 64e-5

GDN_COLS = 6 * WIDTH
RWKV_COLS = 3 * WIDTH + LANES + GATE_LORA_PAD
GATE_COLS = 2 * D_MODEL

VMEM_LIMIT = 56 * 1024 * 1024


def _dot(a, b):
    return jnp.dot(a.astype(BF16), b.astype(BF16), preferred_element_type=F32)


def _dot_nt(a, b):
    return lax.dot_general(a.astype(BF16), b.astype(BF16), (((1,), (1,)), ((), ())),
                           preferred_element_type=F32)


def _dot_tn(a, b):
    return lax.dot_general(a.astype(BF16), b.astype(BF16), (((0,), (0,)), ((), ())),
                           preferred_element_type=F32)


def _split(x):
    hi = x.astype(BF16)
    lo = (x - hi.astype(F32)).astype(BF16)
    return hi, lo


def _dot2(a, b):
    hi, lo = _split(a)
    return (jnp.dot(hi, b, preferred_element_type=F32) + jnp.dot(lo, b, preferred_element_type=F32))


def _dot2_left(a, b):
    hi, lo = _split(b)
    return (jnp.dot(a, hi, preferred_element_type=F32) + jnp.dot(a, lo, preferred_element_type=F32))


def _silu(x):
    return x * jax.nn.sigmoid(x)


def _softplus(x):
    return jnp.maximum(x, 0.0) + jnp.log(1.0 + jnp.exp(-jnp.abs(x)))


def _rms(x, eps):
    return x * lax.rsqrt(jnp.mean(x * x, axis=-1, keepdims=True) + eps)


def _shift_rows(x, s, prev):
    rolled = pltpu.roll(x, s, axis=0)
    prev_rolled = pltpu.roll(prev, s, axis=0)
    row = lax.broadcasted_iota(jnp.int32, prev.shape, 0)
    top = jnp.where(row < s, prev_rolled, rolled[:SUBLANES])
    return jnp.concatenate([top, rolled[SUBLANES:]], axis=0)


def _pair_iotas(shape):
    row = lax.broadcasted_iota(jnp.int32, shape, 0)
    lane = lax.broadcasted_iota(jnp.int32, shape, 1)
    return row, lane


def _bd_mask():
    row, lane = _pair_iotas((LANES, LANES))
    return (row >> HEAD_SHIFT) == (lane >> HEAD_SHIFT)


def _bd(y, bd):
    return jnp.where(bd, jnp.concatenate([y, y], axis=0), 0.0)


def _head_rows(y):
    _, lane = _pair_iotas(y.shape)
    first = lane < HEAD_DIM
    return jnp.concatenate([jnp.where(first, y, 0.0), jnp.where(first, 0.0, y)], axis=0)


def _inv_unit_lower(low, bd):
    row, lane = _pair_iotas(low.shape)
    col = lane & (HEAD_DIM - 1)
    eye = jnp.where(row == col, 1.0, 0.0)
    l21 = jnp.where((row >> 1) == (col >> 1), low, 0.0)
    t = eye - l21
    for lg in range(1, HEAD_SHIFT):
        same_2s = (row >> (lg + 1)) == (col >> (lg + 1))
        l21 = jnp.where(same_2s, jnp.where((row >> lg) == (col >> lg), 0.0, low), 0.0)
        x = _dot(t, _bd(l21, bd))
        t = t - _dot(x, _bd(t, bd))
    return t


def _seg_sum(x, ones_bd):
    return _dot2(x, ones_bd)


def _mod_kernel(c_ref, w_ref, b_ref, o_ref):
    o_ref[...] = _dot(_silu(c_ref[...]), w_ref[...]) + b_ref[...]


def _mod(c, w_ada, b_ada):
    bsz = c.shape[0]
    n = w_ada.shape[1]
    tn = 1536
    return pl.pallas_call(
        _mod_kernel,
        out_shape=jax.ShapeDtypeStruct((bsz, n), F32),
        grid=(n // tn,),
        in_specs=[pl.BlockSpec((bsz, D_MODEL), lambda j: (0, 0)),
                  pl.BlockSpec((D_MODEL, tn), lambda j: (0, j)),
                  pl.BlockSpec((1, tn), lambda j: (0, j))],
        out_specs=pl.BlockSpec((bsz, tn), lambda j: (0, j)),
        name="mod",
    )(c, w_ada, b_ada.reshape(1, n))


def _inproj_kernel(x_ref, sc_ref, sh_ref, nw_ref, wg_ref, wr_ref, wt_ref, og_ref, or_ref, ot_ref):
    h = _rms(x_ref[0], NORM_EPS) * nw_ref[...] * (1.0 + sc_ref[0]) + sh_ref[0]
    hb = h.astype(BF16)
    step = 512
    for w_ref, o_ref in ((wg_ref, og_ref), (wr_ref, or_ref), (wt_ref, ot_ref)):
        n = w_ref.shape[1]
        for j in range(0, n, step):
            e = min(j + step, n)
            o_ref[0, :, j:e] = jnp.dot(hb, w_ref[:, j:e], preferred_element_type=F32).astype(BF16)


def _inproj(x, scale1, shift1, norm1_w, wg, wr, wt, tm):
    bsz, seq, _ = x.shape
    full = lambda a: pl.BlockSpec(a.shape, lambda b, t: (0,) * a.ndim)
    vec = pl.BlockSpec((1, 1, D_MODEL), lambda b, t: (b, 0, 0))
    tile = lambda n: pl.BlockSpec((1, tm, n), lambda b, t: (b, t, 0))
    return pl.pallas_call(
        _inproj_kernel,
        out_shape=(jax.ShapeDtypeStruct((bsz, seq, GDN_COLS), BF16),
                   jax.ShapeDtypeStruct((bsz, seq, RWKV_COLS), BF16),
                   jax.ShapeDtypeStruct((bsz, seq, GATE_COLS), BF16)),
        grid=(bsz, seq // tm),
        in_specs=[tile(D_MODEL), vec, vec, full(norm1_w), full(wg), full(wr), full(wt)],
        out_specs=(tile(GDN_COLS), tile(RWKV_COLS), tile(GATE_COLS)),
        compiler_params=pltpu.CompilerParams(
            dimension_semantics=("parallel", "parallel"), vmem_limit_bytes=VMEM_LIMIT),
        name="inproj",
    )(x, scale1, shift1, norm1_w, wg, wr, wt)


def _gdn_kernel(p_ref, cw_ref, alog_ref, dtb_ref, onw_ref, ones_ref, cum_ref, o_ref,
                tail_ref, s_ref, q_s, k_s, v_s, b_s, g_s, o_s):
    tb = p_ref.shape[1]

    @pl.when(pl.program_id(1) == 0)
    def _():
        tail_ref[...] = jnp.zeros_like(tail_ref)
        s_ref[...] = jnp.zeros_like(s_ref)

    ones_bd = ones_ref[...]
    xin = p_ref[0, :, 0:3 * WIDTH].astype(F32)
    prev = tail_ref[...]
    cw = cw_ref[...]
    conv = xin * cw[GDN_CONV - 1:GDN_CONV]
    for s in range(1, GDN_CONV):
        conv = conv + _shift_rows(xin, s, prev) * cw[GDN_CONV - 1 - s:GDN_CONV - s]
    tail_ref[...] = xin[tb - SUBLANES:]
    qkv = _silu(conv)
    q = qkv[:, 0:WIDTH]
    k = qkv[:, WIDTH:2 * WIDTH]
    q_s[...] = q * lax.rsqrt(_seg_sum(q * q, ones_bd) + 1e-6) * (HEAD_DIM ** -0.5)
    k_s[...] = k * lax.rsqrt(_seg_sum(k * k, ones_bd) + 1e-6)
    v_s[...] = qkv[:, 2 * WIDTH:3 * WIDTH]
    b_s[...] = jax.nn.sigmoid(p_ref[0, :, 4 * WIDTH:5 * WIDTH].astype(F32))
    g_s[...] = -jnp.exp(alog_ref[...]) * _softplus(p_ref[0, :, 5 * WIDTH:6 * WIDTH].astype(F32) + dtb_ref[...])

    bd = _bd_mask()
    row, lane = _pair_iotas((CHUNK, LANES))
    col = lane & (HEAD_DIM - 1)
    causal = row >= col
    strict = row > col
    rowf, lanef = _pair_iotas((CHUNK, WIDTH))
    colf = lanef & (HEAD_DIM - 1)
    causal_f = rowf >= colf
    upper_f = rowf <= colf
    cum_mat = cum_ref[...]

    def chunk_body(c, carry):
        rows = pl.ds(pl.multiple_of(c * CHUNK, CHUNK), CHUNK)
        g = g_s[rows, :]
        gx = jnp.concatenate([g, jnp.where(upper_f, g, 0.0)], axis=0)
        dc = _dot2_left(cum_mat, gx)
        diff = dc[:CHUNK]
        gc = dc[CHUNK:]
        dec = jnp.where(causal_f, jnp.exp(jnp.where(causal_f, diff, 0.0)), 0.0)
        e_gc = jnp.exp(gc)
        g_last = gc[CHUNK - 1:CHUNK]
        e_rem = jnp.exp(g_last - gc)
        e_last = jnp.exp(g_last)
        for p in range(PAIRS):
            ln = slice(p * LANES, (p + 1) * LANES)
            q2 = q_s[rows, ln]
            k2 = k_s[rows, ln]
            v2 = v_s[rows, ln]
            b2 = b_s[rows, ln]
            dec2 = dec[:, ln]
            kb2 = k2 * b2
            vb2 = v2 * b2
            a = _dot_nt(jnp.concatenate([kb2, q2], axis=0), _head_rows(k2))
            low = jnp.where(strict, a[:CHUNK] * dec2, 0.0)
            attn = jnp.where(causal, a[CHUNK:] * dec2, 0.0)
            t = _inv_unit_lower(low, bd)
            uw = _dot(t, jnp.concatenate([_bd(vb2, bd), _bd(kb2 * e_gc[:, ln], bd)], axis=1))
            u2 = uw[:, :LANES]
            w2 = uw[:, LANES:]
            st = s_ref[p]
            v_new = u2 - _dot(w2, st)
            o2 = _dot(q2 * e_gc[:, ln], st) + _dot(attn, _bd(v_new, bd))
            k_dec = k2 * e_rem[:, ln]
            s_ref[p] = st * e_last[:, ln] + jnp.where(bd, _dot_tn(k_dec, v_new), 0.0)
            o_s[rows, ln] = o2
        return carry

    lax.fori_loop(0, tb // CHUNK, chunk_body, 0)

    o = o_s[...]
    z = p_ref[0, :, 3 * WIDTH:4 * WIDTH].astype(F32)
    ms = _seg_sum(o * o, ones_bd) * (1.0 / HEAD_DIM)
    o_ref[0] = (o * lax.rsqrt(ms + NORM_EPS) * onw_ref[...] * _silu(z)).astype(BF16)


def _gdn(pg, conv_w, alog_b, dtb_b, onw_b, ones_bd, cum_mat, tb):
    bsz, seq, _ = pg.shape
    full = lambda a: pl.BlockSpec(a.shape, lambda b, t: (0,) * a.ndim)
    return pl.pallas_call(
        _gdn_kernel,
        out_shape=jax.ShapeDtypeStruct((bsz, seq, WIDTH), BF16),
        grid=(bsz, seq // tb),
        in_specs=[pl.BlockSpec((1, tb, GDN_COLS), lambda b, t: (b, t, 0)),
                  full(conv_w), full(alog_b), full(dtb_b), full(onw_b), full(ones_bd), full(cum_mat)],
        out_specs=pl.BlockSpec((1, tb, WIDTH), lambda b, t: (b, t, 0)),
        scratch_shapes=[pltpu.VMEM((SUBLANES, 3 * WIDTH), F32),
                        pltpu.VMEM((PAIRS, LANES, LANES), F32)]
                       + [pltpu.VMEM((tb, WIDTH), F32) for _ in range(6)],
        compiler_params=pltpu.CompilerParams(
            dimension_semantics=("parallel", "arbitrary"), vmem_limit_bytes=VMEM_LIMIT),
        name="gdn",
    )(pg, conv_w, alog_b, dtb_b, onw_b, ones_bd, cum_mat)


def _rwkv_kernel(p_ref, mu_ref, w0_ref, a0_ref, kk_ref, ka_ref, rk_ref, lw_ref, lb_ref,
                 wwa_ref, g2_ref, ones_ref, tril_ref, o_ref,
                 tail_ref, s_ref, r_s, k_s, v_s, a_s, b_s, d_s, y_s, bonus_s, gate_s):
    tb = p_ref.shape[1]

    @pl.when(pl.program_id(1) == 0)
    def _():
        tail_ref[...] = jnp.zeros_like(tail_ref)
        s_ref[...] = jnp.zeros_like(s_ref)

    ones_bd = ones_ref[...]
    pb = p_ref[0].astype(F32)
    sh = _shift_rows(pb, 1, tail_ref[...])
    tail_ref[...] = pb[tb - SUBLANES:]
    pb = pb + (sh - pb) * mu_ref[...]
    r = pb[:, 0:WIDTH]
    k = pb[:, WIDTH:2 * WIDTH]
    v = pb[:, 2 * WIDTH:3 * WIDTH]
    lo = pb[:, 3 * WIDTH:3 * WIDTH + LANES]
    lane = lax.broadcasted_iota(jnp.int32, lo.shape, 1)
    lo = jnp.where(lane < DECAY_LORA, jnp.tanh(lo), lo)
    wa = _dot(lo, wwa_ref[...])
    w = -_softplus(-(w0_ref[...] + wa[:, :WIDTH])) - 0.5
    a = jax.nn.sigmoid(a0_ref[...] + wa[:, WIDTH:])
    gate_s[...] = _dot(jax.nn.sigmoid(pb[:, 3 * WIDTH + LANES:]), g2_ref[...])
    kk = k * kk_ref[...]
    kk = kk * lax.rsqrt(_seg_sum(kk * kk, ones_bd) + 1e-6)
    k = k * (1.0 + (a - 1.0) * ka_ref[...])
    bonus_s[...] = _seg_sum(r * k * rk_ref[...], ones_bd) * v
    r_s[...] = r
    k_s[...] = k
    v_s[...] = v
    a_s[...] = -kk
    b_s[...] = kk * a
    d_s[...] = -jnp.exp(w)

    bd = _bd_mask()
    row, lane2 = _pair_iotas((CHUNK, LANES))
    col = lane2 & (HEAD_DIM - 1)
    incl = row >= col
    strict = row > col
    tril = tril_ref[...]

    def chunk_body(c, carry):
        rows = pl.ds(pl.multiple_of(c * CHUNK, CHUNK), CHUNK)
        lw = d_s[rows, :]
        lc = _dot2_left(tril, lw)
        lp = lc - lw
        l_last = lc[CHUNK - 1:CHUNK]
        e_neg = jnp.exp(-lc)
        e_rem = jnp.exp(l_last - lc)
        at = a_s[rows, :] * jnp.exp(lp)
        rt = r_s[rows, :] * jnp.exp(lc)
        bv = b_s[rows, :]
        kv = k_s[rows, :]
        bt = bv * e_neg
        kt = kv * e_neg
        bh = bv * e_rem
        kh = kv * e_rem
        e_last = jnp.exp(l_last)
        for p in range(PAIRS):
            ln = slice(p * LANES, (p + 1) * LANES)
            v2 = v_s[rows, ln]
            at2 = at[:, ln]
            rt2 = rt[:, ln]
            sc = _dot_nt(jnp.concatenate([at2, rt2], axis=0),
                         jnp.concatenate([_head_rows(bt[:, ln]), _head_rows(kt[:, ln])], axis=0))
            a_ab = jnp.where(strict, sc[:CHUNK, :LANES], 0.0)
            a_ak = jnp.where(strict, sc[:CHUNK, LANES:], 0.0)
            a_rb = jnp.where(incl, sc[CHUNK:, :LANES], 0.0)
            a_rk = jnp.where(incl, sc[CHUNK:, LANES:], 0.0)
            t = _inv_unit_lower(-a_ab, bd)
            st = s_ref[p]
            v_bd = _bd(v2, bd)
            x = _dot_nt(at2, st) + _dot(a_ak, v_bd)
            u = _dot(t, _bd(x, bd))
            y = _dot_nt(rt2, st) + _dot(a_rb, _bd(u, bd)) + _dot(a_rk, v_bd)
            upd = _dot_tn(jnp.concatenate([u, v2], axis=0),
                          jnp.concatenate([bh[:, ln], kh[:, ln]], axis=0))
            s_ref[p] = st * e_last[:, ln] + jnp.where(bd, upd, 0.0)
            y_s[rows, ln] = y
        return carry

    lax.fori_loop(0, tb // CHUNK, chunk_body, 0)

    y = y_s[...]
    mean = _seg_sum(y, ones_bd) * (1.0 / HEAD_DIM)
    yc = y - mean
    var = _seg_sum(yc * yc, ones_bd) * (1.0 / HEAD_DIM)
    yn = yc * lax.rsqrt(var + LNX_EPS) * lw_ref[...] + lb_ref[...]
    o_ref[0] = ((yn + bonus_s[...]) * gate_s[...]).astype(BF16)


def _rwkv(pr, mu, w0, a0, k_k, k_a, r_k, lnx_w, lnx_b, wwa, g2p, ones_bd, tril, tb):
    bsz, seq, _ = pr.shape
    full = lambda a: pl.BlockSpec(a.shape, lambda b, t: (0,) * a.ndim)
    args = (pr, mu, w0, a0, k_k, k_a, r_k, lnx_w, lnx_b, wwa, g2p, ones_bd, tril)
    return pl.pallas_call(
        _rwkv_kernel,
        out_shape=jax.ShapeDtypeStruct((bsz, seq, WIDTH), BF16),
        grid=(bsz, seq // tb),
        in_specs=[pl.BlockSpec((1, tb, RWKV_COLS), lambda b, t: (b, t, 0))] + [full(a) for a in args[1:]],
        out_specs=pl.BlockSpec((1, tb, WIDTH), lambda b, t: (b, t, 0)),
        scratch_shapes=[pltpu.VMEM((SUBLANES, RWKV_COLS), F32),
                        pltpu.VMEM((PAIRS, LANES, LANES), F32)]
                       + [pltpu.VMEM((tb, WIDTH), F32) for _ in range(9)],
        compiler_params=pltpu.CompilerParams(
            dimension_semantics=("parallel", "arbitrary"), vmem_limit_bytes=VMEM_LIMIT),
        name="rwkv",
    )(*args)


def _merge_kernel(x_ref, ya_ref, yb_ref, gt_ref, g1_ref, wa_ref, wb_ref, wo_ref, o_ref):
    ya = jnp.dot(ya_ref[0], wa_ref[...], preferred_element_type=F32)
    yb = jnp.dot(yb_ref[0], wb_ref[...], preferred_element_type=F32)
    gt = gt_ref[0].astype(F32)
    merged = jax.nn.sigmoid(gt[:, :D_MODEL]) * ya + jax.nn.sigmoid(gt[:, D_MODEL:]) * yb
    o_ref[0] = x_ref[0] + g1_ref[0] * _dot(merged, wo_ref[...])


def _merge(x, ya, yb, gates, gate1, wa, wb, wo, tm):
    bsz, seq, _ = x.shape
    full = lambda a: pl.BlockSpec(a.shape, lambda b, t: (0,) * a.ndim)
    vec = pl.BlockSpec((1, 1, D_MODEL), lambda b, t: (b, 0, 0))
    tile = lambda n: pl.BlockSpec((1, tm, n), lambda b, t: (b, t, 0))
    return pl.pallas_call(
        _merge_kernel,
        out_shape=jax.ShapeDtypeStruct((bsz, seq, D_MODEL), F32),
        grid=(bsz, seq // tm),
        in_specs=[tile(D_MODEL), tile(WIDTH), tile(WIDTH), tile(GATE_COLS), vec, full(wa), full(wb), full(wo)],
        out_specs=tile(D_MODEL),
        compiler_params=pltpu.CompilerParams(
            dimension_semantics=("parallel", "parallel"), vmem_limit_bytes=VMEM_LIMIT),
        name="merge",
    )(x, ya, yb, gates, gate1, wa, wb, wo)


def _ffn_kernel(x_ref, sc_ref, sh_ref, g2_ref, nw_ref, nf_ref, wg_ref, wu_ref, cw_ref, wo_ref, o_ref,
                tail_ref):
    tm = x_ref.shape[1]

    @pl.when(pl.program_id(1) == 0)
    def _():
        tail_ref[...] = jnp.zeros_like(tail_ref)

    x1 = x_ref[0]
    h = _rms(x1, NORM_EPS) * nw_ref[...] * (1.0 + sc_ref[0]) + sh_ref[0]
    hb = h.astype(BF16)
    step = 256
    acc = jnp.zeros((tm, D_MODEL), F32)
    for j in range(0, D_FF, step):
        cs = slice(j, j + step)
        gate = jnp.dot(hb, wg_ref[:, cs], preferred_element_type=F32)
        up = jnp.dot(hb, wu_ref[:, cs], preferred_element_type=F32)
        prev = tail_ref[:, cs]
        cw = cw_ref[:, cs]
        conv = gate * cw[FFN_CONV - 1:FFN_CONV]
        for s in range(1, FFN_CONV):
            conv = conv + _shift_rows(gate, s, prev) * cw[FFN_CONV - 1 - s:FFN_CONV - s]
        tail_ref[:, cs] = gate[tm - SUBLANES:]
        acc = acc + _dot(_silu(conv) * up, wo_ref[cs, :])
    x2 = x1 + g2_ref[0] * acc
    o_ref[0] = _rms(x2, NORM_EPS) * nf_ref[...]


def _ffn(x1, scale2, shift2, gate2, norm2_w, norm_f_w, wg, wu, conv_w, wo, tm):
    bsz, seq, _ = x1.shape
    full = lambda a: pl.BlockSpec(a.shape, lambda b, t: (0,) * a.ndim)
    vec = pl.BlockSpec((1, 1, D_MODEL), lambda b, t: (b, 0, 0))
    tile = pl.BlockSpec((1, tm, D_MODEL), lambda b, t: (b, t, 0))
    return pl.pallas_call(
        _ffn_kernel,
        out_shape=jax.ShapeDtypeStruct((bsz, seq, D_MODEL), F32),
        grid=(bsz, seq // tm),
        in_specs=[tile, vec, vec, vec, full(norm2_w), full(norm_f_w), full(wg), full(wu), full(conv_w), full(wo)],
        out_specs=tile,
        scratch_shapes=[pltpu.VMEM((SUBLANES, D_FF), F32)],
        compiler_params=pltpu.CompilerParams(
            dimension_semantics=("parallel", "arbitrary"), vmem_limit_bytes=VMEM_LIMIT),
        name="ffn",
    )(x1, scale2, shift2, gate2, norm2_w, norm_f_w, wg, wu, conv_w, wo)


def _rep_heads(p):
    return jnp.repeat(p.reshape(1, HEADS), HEAD_DIM, axis=1)


def _const_mats():
    i = jnp.arange(LANES)
    tril64 = (jnp.arange(CHUNK)[:, None] >= jnp.arange(CHUNK)[None, :]).astype(F32)
    ones64 = jnp.ones((CHUNK, CHUNK), F32)
    cum = jnp.concatenate([jnp.concatenate([tril64, -ones64], axis=1),
                           jnp.concatenate([tril64, jnp.zeros_like(ones64)], axis=1)], axis=0)
    w = jnp.arange(WIDTH)
    ones_bd = ((w[:, None] // HEAD_DIM) == (w[None, :] // HEAD_DIM)).astype(BF16)
    del i
    return ones_bd, cum.astype(BF16), tril64.astype(BF16)


def kernel(x, c, w_ada, b_ada, norm1_w, w_in, conv_gdn, a_log, dt_bias, onorm_gdn, w_branch_gdn, mu_rwkv, w0, w2, a0, a2, g2, k_k, k_a, r_k, lnx_w, lnx_b, w_branch_rwkv, w_out, norm2_w, w_ffn_in, conv_ffn, w_ffn_out, norm_f_w):
    bsz, seq, _ = x.shape
    tm = min(256, seq)
    tb = min(256, seq)
    ones_bd, cum_mat, tril = _const_mats()

    mod = _mod(c, w_ada[0], b_ada[0])
    shift1, scale1, gate1, shift2, scale2, gate2 = [m.reshape(bsz, 1, D_MODEL) for m in jnp.split(mod, 6, axis=-1)]

    wi = w_in[0]
    o_z = 3 * WIDTH
    o_b = o_z + WIDTH
    o_a = o_b + HEADS
    o_r = o_a + HEADS
    o_lo = o_r + 3 * WIDTH
    o_g = o_lo + DECAY_LORA + ICLR_LORA
    o_t = o_g + GATE_LORA
    wg = jnp.concatenate([wi[:, :o_b],
                          jnp.repeat(wi[:, o_b:o_a], HEAD_DIM, axis=1),
                          jnp.repeat(wi[:, o_a:o_r], HEAD_DIM, axis=1)], axis=1).astype(BF16)
    pad = GATE_LORA_PAD - GATE_LORA
    wr = jnp.concatenate([wi[:, o_r:o_t], jnp.zeros((D_MODEL, pad), F32)], axis=1).astype(BF16)
    wt = wi[:, o_t:].astype(BF16)
    pg, pr, gates = _inproj(x, scale1, shift1, norm1_w, wg, wr, wt, tm)

    y_a = _gdn(pg, conv_gdn[0], _rep_heads(a_log[0]), _rep_heads(dt_bias[0]),
               jnp.tile(onorm_gdn, (1, HEADS)), ones_bd, cum_mat, tb)

    mu = jnp.concatenate([mu_rwkv[0], jnp.zeros((pad,), F32)]).reshape(1, RWKV_COLS)
    wwa = jnp.zeros((LANES, 2 * WIDTH), F32)
    wwa = wwa.at[:DECAY_LORA, :WIDTH].set(w2[0]).at[DECAY_LORA:, WIDTH:].set(a2[0]).astype(BF16)
    g2p = jnp.concatenate([g2[0], jnp.zeros((pad, WIDTH), F32)], axis=0).astype(BF16)
    y_b = _rwkv(pr, mu, w0, a0, k_k, k_a, r_k.reshape(1, WIDTH), lnx_w, lnx_b, wwa, g2p, ones_bd, tril, tb)

    x1 = _merge(x, y_a, y_b, gates, gate1, w_branch_gdn[0].astype(BF16), w_branch_rwkv[0].astype(BF16),
                w_out[0].astype(BF16), tm)

    wf = w_ffn_in[0]
    return _ffn(x1, scale2, shift2, gate2, norm2_w, norm_f_w.reshape(1, D_MODEL),
                wf[:, :D_FF].astype(BF16), wf[:, D_FF:].astype(BF16), conv_ffn[0], w_ffn_out[0].astype(BF16), tm)
```

```python
import functools

import jax
import jax.numpy as jnp
from jax import lax
from jax.experimental import pallas as pl
from jax.experimental.pallas import tpu as pltpu

F32 = jnp.float32
BF16 = jnp.bfloat16

D_MODEL = 1024
HEADS = 8
HEAD_DIM = 64
HEAD_SHIFT = 6
WIDTH = HEADS * HEAD_DIM
PAIRS = HEADS // 2
LANES = 128
SUBLANES = 8
CHUNK = 64
GDN_CONV = 4
DECAY_LORA = 64
ICLR_LORA = 64
GATE_LORA = 160
GATE_LORA_PAD = 256
D_FF = 2816
FFN_CONV = 3
NORM_EPS = 1e-6
LNX_EPS = 64e-5

GDN_COLS = 6 * WIDTH
RWKV_COLS = 3 * WIDTH + LANES + GATE_LORA_PAD
GATE_COLS = 2 * D_MODEL

VMEM_LIMIT = 56 * 1024 * 1024


def _dot(a, b):
    return jnp.dot(a.astype(BF16), b.astype(BF16), preferred_element_type=F32)


def _dot_nt(a, b):
    return lax.dot_general(a.astype(BF16), b.astype(BF16), (((1,), (1,)), ((), ())),
                           preferred_element_type=F32)


def _dot_tn(a, b):
    return lax.dot_general(a.astype(BF16), b.astype(BF16), (((0,), (0,)), ((), ())),
                           preferred_element_type=F32)


def _split(x):
    hi = x.astype(BF16)
    lo = (x - hi.astype(F32)).astype(BF16)
    return hi, lo


def _dot2(a, b):
    hi, lo = _split(a)
    return (jnp.dot(hi, b, preferred_element_type=F32) + jnp.dot(lo, b, preferred_element_type=F32))


def _dot2_left(a, b):
    hi, lo = _split(b)
    return (jnp.dot(a, hi, preferred_element_type=F32) + jnp.dot(a, lo, preferred_element_type=F32))


def _silu(x):
    return x * jax.nn.sigmoid(x)


def _softplus(x):
    return jnp.maximum(x, 0.0) + jnp.log(1.0 + jnp.exp(-jnp.abs(x)))


def _rms(x, eps):
    return x * lax.rsqrt(jnp.mean(x * x, axis=-1, keepdims=True) + eps)


def _shift_rows(x, s, prev):
    rolled = pltpu.roll(x, s, axis=0)
    prev_rolled = pltpu.roll(prev, s, axis=0)
    row = lax.broadcasted_iota(jnp.int32, prev.shape, 0)
    top = jnp.where(row < s, prev_rolled, rolled[:SUBLANES])
    return jnp.concatenate([top, rolled[SUBLANES:]], axis=0)


def _pair_iotas(shape):
    row = lax.broadcasted_iota(jnp.int32, shape, 0)
    lane = lax.broadcasted_iota(jnp.int32, shape, 1)
    return row, lane


def _bd_mask():
    row, lane = _pair_iotas((LANES, LANES))
    return (row >> HEAD_SHIFT) == (lane >> HEAD_SHIFT)


def _bd(y, bd):
    return jnp.where(bd, jnp.concatenate([y, y], axis=0), 0.0)


def _head_rows(y):
    _, lane = _pair_iotas(y.shape)
    first = lane < HEAD_DIM
    return jnp.concatenate([jnp.where(first, y, 0.0), jnp.where(first, 0.0, y)], axis=0)


def _inv_unit_lower(lows, bd):
    row, lane = _pair_iotas((CHUNK, LANES))
    col = lane & (HEAD_DIM - 1)
    eye = jnp.where(row == col, 1.0, 0.0)
    first = (row >> 1) == (col >> 1)
    ts = [eye - jnp.where(first, low, 0.0) for low in lows]
    for lg in range(1, HEAD_SHIFT):
        off_diag = ((row >> (lg + 1)) == (col >> (lg + 1))) & ((row >> lg) != (col >> lg))
        xs = [_dot(t, _bd(jnp.where(off_diag, low, 0.0), bd)) for t, low in zip(ts, lows)]
        ts = [t - _dot(x, _bd(t, bd)) for t, x in zip(ts, xs)]
    return ts


def _seg_sum(x, ones_bd):
    return _dot2(x, ones_bd)


def _mod_kernel(c_ref, w_ref, b_ref, o_ref):
    o_ref[...] = _dot(_silu(c_ref[...]), w_ref[...]) + b_ref[...]


def _mod(c, w_ada, b_ada):
    bsz = c.shape[0]
    n = w_ada.shape[1]
    tn = 1536
    return pl.pallas_call(
        _mod_kernel,
        out_shape=jax.ShapeDtypeStruct((bsz, n), F32),
        grid=(n // tn,),
        in_specs=[pl.BlockSpec((bsz, D_MODEL), lambda j: (0, 0)),
                  pl.BlockSpec((D_MODEL, tn), lambda j: (0, j)),
                  pl.BlockSpec((1, tn), lambda j: (0, j))],
        out_specs=pl.BlockSpec((bsz, tn), lambda j: (0, j)),
        name="mod",
    )(c, w_ada, b_ada.reshape(1, n))


def _inproj_kernel(x_ref, sc_ref, sh_ref, nw_ref, wg_ref, wr_ref, wt_ref, og_ref, or_ref, ot_ref):
    h = _rms(x_ref[0], NORM_EPS) * nw_ref[...] * (1.0 + sc_ref[0]) + sh_ref[0]
    hb = h.astype(BF16)
    step = 512
    for w_ref, o_ref in ((wg_ref, og_ref), (wr_ref, or_ref), (wt_ref, ot_ref)):
        n = w_ref.shape[1]
        for j in range(0, n, step):
            e = min(j + step, n)
            o_ref[0, :, j:e] = jnp.dot(hb, w_ref[:, j:e], preferred_element_type=F32).astype(BF16)


def _inproj(x, scale1, shift1, norm1_w, wg, wr, wt, tm):
    bsz, seq, _ = x.shape
    full = lambda a: pl.BlockSpec(a.shape, lambda b, t: (0,) * a.ndim)
    vec = pl.BlockSpec((1, 1, D_MODEL), lambda b, t: (b, 0, 0))
    tile = lambda n: pl.BlockSpec((1, tm, n), lambda b, t: (b, t, 0))
    return pl.pallas_call(
        _inproj_kernel,
        out_shape=(jax.ShapeDtypeStruct((bsz, seq, GDN_COLS), BF16),
                   jax.ShapeDtypeStruct((bsz, seq, RWKV_COLS), BF16),
                   jax.ShapeDtypeStruct((bsz, seq, GATE_COLS), BF16)),
        grid=(bsz, seq // tm),
        in_specs=[tile(D_MODEL), vec, vec, full(norm1_w), full(wg), full(wr), full(wt)],
        out_specs=(tile(GDN_COLS), tile(RWKV_COLS), tile(GATE_COLS)),
        compiler_params=pltpu.CompilerParams(
            dimension_semantics=("parallel", "parallel"), vmem_limit_bytes=VMEM_LIMIT),
        name="inproj",
    )(x, scale1, shift1, norm1_w, wg, wr, wt)


def _gdn_kernel(p_ref, cw_ref, alog_ref, dtb_ref, onw_ref, ones_ref, cum_ref, o_ref,
                tail_ref, s_ref, q_s, k_s, v_s, b_s, g_s, o_s, dec_s, egc_s, erem_s):
    tb = p_ref.shape[1]

    @pl.when(pl.program_id(1) == 0)
    def _():
        tail_ref[...] = jnp.zeros_like(tail_ref)
        s_ref[...] = jnp.zeros_like(s_ref)

    ones_bd = ones_ref[...]
    xin = p_ref[0, :, 0:3 * WIDTH].astype(F32)
    prev = tail_ref[...]
    cw = cw_ref[...]
    conv = xin * cw[GDN_CONV - 1:GDN_CONV]
    for s in range(1, GDN_CONV):
        conv = conv + _shift_rows(xin, s, prev) * cw[GDN_CONV - 1 - s:GDN_CONV - s]
    tail_ref[...] = xin[tb - SUBLANES:]
    qkv = _silu(conv)
    q = qkv[:, 0:WIDTH]
    k = qkv[:, WIDTH:2 * WIDTH]
    q_s[...] = q * lax.rsqrt(_seg_sum(q * q, ones_bd) + 1e-6) * (HEAD_DIM ** -0.5)
    k_s[...] = k * lax.rsqrt(_seg_sum(k * k, ones_bd) + 1e-6)
    v_s[...] = qkv[:, 2 * WIDTH:3 * WIDTH]
    b_s[...] = jax.nn.sigmoid(p_ref[0, :, 4 * WIDTH:5 * WIDTH].astype(F32))
    g_s[...] = -jnp.exp(alog_ref[...]) * _softplus(p_ref[0, :, 5 * WIDTH:6 * WIDTH].astype(F32) + dtb_ref[...])

    bd = _bd_mask()
    row, lane = _pair_iotas((CHUNK, LANES))
    col = lane & (HEAD_DIM - 1)
    causal = row >= col
    strict = row > col
    rowf, lanef = _pair_iotas((CHUNK, WIDTH))
    colf = lanef & (HEAD_DIM - 1)
    causal_f = rowf >= colf
    upper_f = rowf <= colf
    cum_mat = cum_ref[...]

    nch = tb // CHUNK

    def blk(ref, c, p):
        return ref[c * CHUNK:(c + 1) * CHUNK, p * LANES:(p + 1) * LANES]

    e_last = []
    for c in range(nch):
        rows = slice(c * CHUNK, (c + 1) * CHUNK)
        g = g_s[rows, :]
        gx = jnp.concatenate([g, jnp.where(upper_f, g, 0.0)], axis=0)
        dc = _dot2_left(cum_mat, gx)
        diff = dc[:CHUNK]
        gc = dc[CHUNK:]
        dec_s[rows, :] = jnp.where(causal_f, jnp.exp(jnp.where(causal_f, diff, 0.0)), 0.0)
        egc_s[rows, :] = jnp.exp(gc)
        g_last = gc[CHUNK - 1:CHUNK]
        erem_s[rows, :] = jnp.exp(g_last - gc)
        e_last.append(jnp.exp(g_last))

    probs = [(c, p) for c in range(nch) for p in range(PAIRS)]
    lows, attns = [], []
    for c, p in probs:
        k2 = blk(k_s, c, p)
        a = _dot_nt(jnp.concatenate([k2 * blk(b_s, c, p), blk(q_s, c, p)], axis=0), _head_rows(k2))
        dec2 = blk(dec_s, c, p)
        lows.append(jnp.where(strict, a[:CHUNK] * dec2, 0.0))
        attns.append(jnp.where(causal, a[CHUNK:] * dec2, 0.0))
    ts = _inv_unit_lower(lows, bd)
    uws = []
    for (c, p), t in zip(probs, ts):
        b2 = blk(b_s, c, p)
        rhs = jnp.concatenate([_bd(blk(v_s, c, p) * b2, bd),
                               _bd(blk(k_s, c, p) * b2 * blk(egc_s, c, p), bd)], axis=1)
        uws.append(_dot(t, rhs))

    state = [s_ref[p] for p in range(PAIRS)]
    for c in range(nch):
        base = c * PAIRS
        ws = [_dot(jnp.concatenate([uws[base + p][:, LANES:], blk(q_s, c, p) * blk(egc_s, c, p)], axis=0),
                   state[p]) for p in range(PAIRS)]
        v_new = [uws[base + p][:, :LANES] - ws[p][:CHUNK] for p in range(PAIRS)]
        o_in = [_dot(attns[base + p], _bd(v_new[p], bd)) for p in range(PAIRS)]
        upd = [_dot_tn(blk(k_s, c, p) * blk(erem_s, c, p), v_new[p]) for p in range(PAIRS)]
        for p in range(PAIRS):
            ln = slice(p * LANES, (p + 1) * LANES)
            state[p] = state[p] * e_last[c][:, ln] + jnp.where(bd, upd[p], 0.0)
            o_s[c * CHUNK:(c + 1) * CHUNK, ln] = ws[p][CHUNK:] + o_in[p]
    for p in range(PAIRS):
        s_ref[p] = state[p]

    o = o_s[...]
    z = p_ref[0, :, 3 * WIDTH:4 * WIDTH].astype(F32)
    ms = _seg_sum(o * o, ones_bd) * (1.0 / HEAD_DIM)
    o_ref[0] = (o * lax.rsqrt(ms + NORM_EPS) * onw_ref[...] * _silu(z)).astype(BF16)


def _gdn(pg, conv_w, alog_b, dtb_b, onw_b, ones_bd, cum_mat, tb):
    bsz, seq, _ = pg.shape
    full = lambda a: pl.BlockSpec(a.shape, lambda b, t: (0,) * a.ndim)
    return pl.pallas_call(
        _gdn_kernel,
        out_shape=jax.ShapeDtypeStruct((bsz, seq, WIDTH), BF16),
        grid=(bsz, seq // tb),
        in_specs=[pl.BlockSpec((1, tb, GDN_COLS), lambda b, t: (b, t, 0)),
                  full(conv_w), full(alog_b), full(dtb_b), full(onw_b), full(ones_bd), full(cum_mat)],
        out_specs=pl.BlockSpec((1, tb, WIDTH), lambda b, t: (b, t, 0)),
        scratch_shapes=[pltpu.VMEM((SUBLANES, 3 * WIDTH), F32),
                        pltpu.VMEM((PAIRS, LANES, LANES), F32)]
                       + [pltpu.VMEM((tb, WIDTH), F32) for _ in range(9)],
        compiler_params=pltpu.CompilerParams(
            dimension_semantics=("parallel", "arbitrary"), vmem_limit_bytes=VMEM_LIMIT),
        name="gdn",
    )(pg, conv_w, alog_b, dtb_b, onw_b, ones_bd, cum_mat)


def _rwkv_kernel(p_ref, mu_ref, w0_ref, a0_ref, kk_ref, ka_ref, rk_ref, lw_ref, lb_ref,
                 wwa_ref, g2_ref, ones_ref, tril_ref, o_ref,
                 tail_ref, s_ref, r_s, k_s, v_s, a_s, b_s, d_s, y_s, bonus_s, gate_s, bt_s, kt_s):
    tb = p_ref.shape[1]

    @pl.when(pl.program_id(1) == 0)
    def _():
        tail_ref[...] = jnp.zeros_like(tail_ref)
        s_ref[...] = jnp.zeros_like(s_ref)

    ones_bd = ones_ref[...]
    pb = p_ref[0].astype(F32)
    sh = _shift_rows(pb, 1, tail_ref[...])
    tail_ref[...] = pb[tb - SUBLANES:]
    pb = pb + (sh - pb) * mu_ref[...]
    r = pb[:, 0:WIDTH]
    k = pb[:, WIDTH:2 * WIDTH]
    v = pb[:, 2 * WIDTH:3 * WIDTH]
    lo = pb[:, 3 * WIDTH:3 * WIDTH + LANES]
    lane = lax.broadcasted_iota(jnp.int32, lo.shape, 1)
    lo = jnp.where(lane < DECAY_LORA, jnp.tanh(lo), lo)
    wa = _dot(lo, wwa_ref[...])
    w = -_softplus(-(w0_ref[...] + wa[:, :WIDTH])) - 0.5
    a = jax.nn.sigmoid(a0_ref[...] + wa[:, WIDTH:])
    gate_s[...] = _dot(jax.nn.sigmoid(pb[:, 3 * WIDTH + LANES:]), g2_ref[...])
    kk = k * kk_ref[...]
    kk = kk * lax.rsqrt(_seg_sum(kk * kk, ones_bd) + 1e-6)
    k = k * (1.0 + (a - 1.0) * ka_ref[...])
    bonus_s[...] = _seg_sum(r * k * rk_ref[...], ones_bd) * v
    r_s[...] = r
    k_s[...] = k
    v_s[...] = v
    a_s[...] = -kk
    b_s[...] = kk * a
    d_s[...] = -jnp.exp(w)

    bd = _bd_mask()
    row, lane2 = _pair_iotas((CHUNK, LANES))
    col = lane2 & (HEAD_DIM - 1)
    incl = row >= col
    strict = row > col
    tril = tril_ref[...]

    nch = tb // CHUNK

    def blk(ref, c, p):
        return ref[c * CHUNK:(c + 1) * CHUNK, p * LANES:(p + 1) * LANES]

    e_last = []
    for c in range(nch):
        rows = slice(c * CHUNK, (c + 1) * CHUNK)
        lw = d_s[rows, :]
        lc = _dot2_left(tril, lw)
        l_last = lc[CHUNK - 1:CHUNK]
        e_neg = jnp.exp(-lc)
        e_rem = jnp.exp(l_last - lc)
        a_s[rows, :] = a_s[rows, :] * jnp.exp(lc - lw)
        r_s[rows, :] = r_s[rows, :] * jnp.exp(lc)
        bv = b_s[rows, :]
        kv = k_s[rows, :]
        bt_s[rows, :] = bv * e_neg
        kt_s[rows, :] = kv * e_neg
        b_s[rows, :] = bv * e_rem
        k_s[rows, :] = kv * e_rem
        e_last.append(jnp.exp(l_last))

    probs = [(c, p) for c in range(nch) for p in range(PAIRS)]
    lows, a_aks, a_rbs, a_rks = [], [], [], []
    for c, p in probs:
        sc = _dot_nt(jnp.concatenate([blk(a_s, c, p), blk(r_s, c, p)], axis=0),
                     jnp.concatenate([_head_rows(blk(bt_s, c, p)), _head_rows(blk(kt_s, c, p))], axis=0))
        lows.append(jnp.where(strict, -sc[:CHUNK, :LANES], 0.0))
        a_aks.append(jnp.where(strict, sc[:CHUNK, LANES:], 0.0))
        a_rbs.append(jnp.where(incl, sc[CHUNK:, :LANES], 0.0))
        a_rks.append(jnp.where(incl, sc[CHUNK:, LANES:], 0.0))
    ts = _inv_unit_lower(lows, bd)
    avs = [_dot(jnp.concatenate([a_ak, a_rk], axis=0), _bd(blk(v_s, c, p), bd))
           for (c, p), a_ak, a_rk in zip(probs, a_aks, a_rks)]
    tws = [_dot(t, jnp.concatenate([_bd(blk(a_s, c, p), bd), _bd(av[:CHUNK], bd)], axis=1))
           for (c, p), t, av in zip(probs, ts, avs)]

    state = [s_ref[p] for p in range(PAIRS)]
    for c in range(nch):
        base = c * PAIRS
        m1 = [_dot_nt(jnp.concatenate([tws[base + p][:, :LANES], blk(r_s, c, p)], axis=0), state[p])
              for p in range(PAIRS)]
        us = [m1[p][:CHUNK] + tws[base + p][:, LANES:] for p in range(PAIRS)]
        y_in = [_dot(a_rbs[base + p], _bd(us[p], bd)) for p in range(PAIRS)]
        upd = [_dot_tn(jnp.concatenate([us[p], blk(v_s, c, p)], axis=0),
                       jnp.concatenate([blk(b_s, c, p), blk(k_s, c, p)], axis=0)) for p in range(PAIRS)]
        for p in range(PAIRS):
            ln = slice(p * LANES, (p + 1) * LANES)
            state[p] = state[p] * e_last[c][:, ln] + jnp.where(bd, upd[p], 0.0)
            y_s[c * CHUNK:(c + 1) * CHUNK, ln] = m1[p][CHUNK:] + y_in[p] + avs[base + p][CHUNK:]
    for p in range(PAIRS):
        s_ref[p] = state[p]

    y = y_s[...]
    mean = _seg_sum(y, ones_bd) * (1.0 / HEAD_DIM)
    yc = y - mean
    var = _seg_sum(yc * yc, ones_bd) * (1.0 / HEAD_DIM)
    yn = yc * lax.rsqrt(var + LNX_EPS) * lw_ref[...] + lb_ref[...]
    o_ref[0] = ((yn + bonus_s[...]) * gate_s[...]).astype(BF16)


def _rwkv(pr, mu, w0, a0, k_k, k_a, r_k, lnx_w, lnx_b, wwa, g2p, ones_bd, tril, tb):
    bsz, seq, _ = pr.shape
    full = lambda a: pl.BlockSpec(a.shape, lambda b, t: (0,) * a.ndim)
    args = (pr, mu, w0, a0, k_k, k_a, r_k, lnx_w, lnx_b, wwa, g2p, ones_bd, tril)
    return pl.pallas_call(
        _rwkv_kernel,
        out_shape=jax.ShapeDtypeStruct((bsz, seq, WIDTH), BF16),
        grid=(bsz, seq // tb),
        in_specs=[pl.BlockSpec((1, tb, RWKV_COLS), lambda b, t: (b, t, 0))] + [full(a) for a in args[1:]],
        out_specs=pl.BlockSpec((1, tb, WIDTH), lambda b, t: (b, t, 0)),
        scratch_shapes=[pltpu.VMEM((SUBLANES, RWKV_COLS), F32),
                        pltpu.VMEM((PAIRS, LANES, LANES), F32)]
                       + [pltpu.VMEM((tb, WIDTH), F32) for _ in range(11)],
        compiler_params=pltpu.CompilerParams(
            dimension_semantics=("parallel", "arbitrary"), vmem_limit_bytes=VMEM_LIMIT),
        name="rwkv",
    )(*args)


def _merge_kernel(x_ref, ya_ref, yb_ref, gt_ref, g1_ref, wa_ref, wb_ref, wo_ref, o_ref):
    ya = jnp.dot(ya_ref[0], wa_ref[...], preferred_element_type=F32)
    yb = jnp.dot(yb_ref[0], wb_ref[...], preferred_element_type=F32)
    gt = gt_ref[0].astype(F32)
    merged = jax.nn.sigmoid(gt[:, :D_MODEL]) * ya + jax.nn.sigmoid(gt[:, D_MODEL:]) * yb
    o_ref[0] = x_ref[0] + g1_ref[0] * _dot(merged, wo_ref[...])


def _merge(x, ya, yb, gates, gate1, wa, wb, wo, tm):
    bsz, seq, _ = x.shape
    full = lambda a: pl.BlockSpec(a.shape, lambda b, t: (0,) * a.ndim)
    vec = pl.BlockSpec((1, 1, D_MODEL), lambda b, t: (b, 0, 0))
    tile = lambda n: pl.BlockSpec((1, tm, n), lambda b, t: (b, t, 0))
    return pl.pallas_call(
        _merge_kernel,
        out_shape=jax.ShapeDtypeStruct((bsz, seq, D_MODEL), F32),
        grid=(bsz, seq // tm),
        in_specs=[tile(D_MODEL), tile(WIDTH), tile(WIDTH), tile(GATE_COLS), vec, full(wa), full(wb), full(wo)],
        out_specs=tile(D_MODEL),
        compiler_params=pltpu.CompilerParams(
            dimension_semantics=("parallel", "parallel"), vmem_limit_bytes=VMEM_LIMIT),
        name="merge",
    )(x, ya, yb, gates, gate1, wa, wb, wo)


def _ffn_kernel(x_ref, sc_ref, sh_ref, g2_ref, nw_ref, nf_ref, wg_ref, wu_ref, cw_ref, wo_ref, o_ref,
                tail_ref):
    tm = x_ref.shape[1]

    @pl.when(pl.program_id(1) == 0)
    def _():
        tail_ref[...] = jnp.zeros_like(tail_ref)

    x1 = x_ref[0]
    h = _rms(x1, NORM_EPS) * nw_ref[...] * (1.0 + sc_ref[0]) + sh_ref[0]
    hb = h.astype(BF16)
    step = 256
    acc = jnp.zeros((tm, D_MODEL), F32)
    for j in range(0, D_FF, step):
        cs = slice(j, j + step)
        gate = jnp.dot(hb, wg_ref[:, cs], preferred_element_type=F32)
        up = jnp.dot(hb, wu_ref[:, cs], preferred_element_type=F32)
        prev = tail_ref[:, cs]
        cw = cw_ref[:, cs]
        conv = gate * cw[FFN_CONV - 1:FFN_CONV]
        for s in range(1, FFN_CONV):
            conv = conv + _shift_rows(gate, s, prev) * cw[FFN_CONV - 1 - s:FFN_CONV - s]
        tail_ref[:, cs] = gate[tm - SUBLANES:]
        acc = acc + _dot(_silu(conv) * up, wo_ref[cs, :])
    x2 = x1 + g2_ref[0] * acc
    o_ref[0] = _rms(x2, NORM_EPS) * nf_ref[...]


def _ffn(x1, scale2, shift2, gate2, norm2_w, norm_f_w, wg, wu, conv_w, wo, tm):
    bsz, seq, _ = x1.shape
    full = lambda a: pl.BlockSpec(a.shape, lambda b, t: (0,) * a.ndim)
    vec = pl.BlockSpec((1, 1, D_MODEL), lambda b, t: (b, 0, 0))
    tile = pl.BlockSpec((1, tm, D_MODEL), lambda b, t: (b, t, 0))
    return pl.pallas_call(
        _ffn_kernel,
        out_shape=jax.ShapeDtypeStruct((bsz, seq, D_MODEL), F32),
        grid=(bsz, seq // tm),
        in_specs=[tile, vec, vec, vec, full(norm2_w), full(norm_f_w), full(wg), full(wu), full(conv_w), full(wo)],
        out_specs=tile,
        scratch_shapes=[pltpu.VMEM((SUBLANES, D_FF), F32)],
        compiler_params=pltpu.CompilerParams(
            dimension_semantics=("parallel", "arbitrary"), vmem_limit_bytes=VMEM_LIMIT),
        name="ffn",
    )(x1, scale2, shift2, gate2, norm2_w, norm_f_w, wg, wu, conv_w, wo)


def _rep_heads(p):
    return jnp.repeat(p.reshape(1, HEADS), HEAD_DIM, axis=1)


def _const_mats():
    i = jnp.arange(LANES)
    tril64 = (jnp.arange(CHUNK)[:, None] >= jnp.arange(CHUNK)[None, :]).astype(F32)
    ones64 = jnp.ones((CHUNK, CHUNK), F32)
    cum = jnp.concatenate([jnp.concatenate([tril64, -ones64], axis=1),
                           jnp.concatenate([tril64, jnp.zeros_like(ones64)], axis=1)], axis=0)
    w = jnp.arange(WIDTH)
    ones_bd = ((w[:, None] // HEAD_DIM) == (w[None, :] // HEAD_DIM)).astype(BF16)
    del i
    return ones_bd, cum.astype(BF16), tril64.astype(BF16)


def kernel(x, c, w_ada, b_ada, norm1_w, w_in, conv_gdn, a_log, dt_bias, onorm_gdn, w_branch_gdn, mu_rwkv, w0, w2, a0, a2, g2, k_k, k_a, r_k, lnx_w, lnx_b, w_branch_rwkv, w_out, norm2_w, w_ffn_in, conv_ffn, w_ffn_out, norm_f_w):
    bsz, seq, _ = x.shape
    tm = min(256, seq)
    tb = min(256, seq)
    ones_bd, cum_mat, tril = _const_mats()

    mod = _mod(c, w_ada[0], b_ada[0])
    shift1, scale1, gate1, shift2, scale2, gate2 = [m.reshape(bsz, 1, D_MODEL) for m in jnp.split(mod, 6, axis=-1)]

    wi = w_in[0]
    o_z = 3 * WIDTH
    o_b = o_z + WIDTH
    o_a = o_b + HEADS
    o_r = o_a + HEADS
    o_lo = o_r + 3 * WIDTH
    o_g = o_lo + DECAY_LORA + ICLR_LORA
    o_t = o_g + GATE_LORA
    wg = jnp.concatenate([wi[:, :o_b],
                          jnp.repeat(wi[:, o_b:o_a], HEAD_DIM, axis=1),
                          jnp.repeat(wi[:, o_a:o_r], HEAD_DIM, axis=1)], axis=1).astype(BF16)
    pad = GATE_LORA_PAD - GATE_LORA
    wr = jnp.concatenate([wi[:, o_r:o_t], jnp.zeros((D_MODEL, pad), F32)], axis=1).astype(BF16)
    wt = wi[:, o_t:].astype(BF16)
    pg, pr, gates = _inproj(x, scale1, shift1, norm1_w, wg, wr, wt, tm)

    y_a = _gdn(pg, conv_gdn[0], _rep_heads(a_log[0]), _rep_heads(dt_bias[0]),
               jnp.tile(onorm_gdn, (1, HEADS)), ones_bd, cum_mat, tb)

    mu = jnp.concatenate([mu_rwkv[0], jnp.zeros((pad,), F32)]).reshape(1, RWKV_COLS)
    wwa = jnp.zeros((LANES, 2 * WIDTH), F32)
    wwa = wwa.at[:DECAY_LORA, :WIDTH].set(w2[0]).at[DECAY_LORA:, WIDTH:].set(a2[0]).astype(BF16)
    g2p = jnp.concatenate([g2[0], jnp.zeros((pad, WIDTH), F32)], axis=0).astype(BF16)
    y_b = _rwkv(pr, mu, w0, a0, k_k, k_a, r_k.reshape(1, WIDTH), lnx_w, lnx_b, wwa, g2p, ones_bd, tril, tb)

    x1 = _merge(x, y_a, y_b, gates, gate1, w_branch_gdn[0].astype(BF16), w_branch_rwkv[0].astype(BF16),
                w_out[0].astype(BF16), tm)

    wf = w_ffn_in[0]
    return _ffn(x1, scale2, shift2, gate2, norm2_w, norm_f_w.reshape(1, D_MODEL),
                wf[:, :D_FF].astype(BF16), wf[:, D_FF:].astype(BF16), conv_ffn[0], w_ffn_out[0].astype(BF16), tm)
```

```python
import jax
import jax.numpy as jnp
from jax import lax
from jax.experimental import pallas as pl
from jax.experimental.pallas import tpu as pltpu

F32 = jnp.float32
BF16 = jnp.bfloat16

D_MODEL = 1024
HEADS = 8
HEAD_DIM = 64
HEAD_SHIFT = 6
WIDTH = HEADS * HEAD_DIM
PAIRS = HEADS // 2
LANES = 128
SUBLANES = 8
CHUNK = 64
GDN_CONV = 4
DECAY_LORA = 64
ICLR_LORA = 64
GATE_LORA = 160
GATE_LORA_PAD = 256
D_FF = 2816
FFN_CONV = 3
NORM_EPS = 1e-6
LNX_EPS = 64e-5

GDN_COLS = 6 * WIDTH
RWKV_COLS = 3 * WIDTH + LANES + GATE_LORA_PAD
GATE_COLS = 2 * D_MODEL

VMEM_LIMIT = 56 * 1024 * 1024


def _dot(a, b):
    return jnp.dot(a.astype(BF16), b.astype(BF16), preferred_element_type=F32)


def _dot_nt(a, b):
    return lax.dot_general(a.astype(BF16), b.astype(BF16), (((1,), (1,)), ((), ())),
                           preferred_element_type=F32)


def _dot_tn(a, b):
    return lax.dot_general(a.astype(BF16), b.astype(BF16), (((0,), (0,)), ((), ())),
                           preferred_element_type=F32)


def _dot2_left(a, b):
    hi = b.astype(BF16)
    lo = (b - hi.astype(F32)).astype(BF16)
    return (jnp.dot(a, hi, preferred_element_type=F32) + jnp.dot(a, lo, preferred_element_type=F32))


def _resident(a):
    return pl.BlockSpec(a.shape, lambda b, t: (0,) * a.ndim, pipeline_mode=pl.Buffered(1))


def _silu(x):
    return x * jax.nn.sigmoid(x)


def _softplus(x):
    return jnp.maximum(x, 0.0) + jnp.log(1.0 + jnp.exp(-jnp.abs(x)))


def _rms(x, eps):
    return x * lax.rsqrt(jnp.mean(x * x, axis=-1, keepdims=True) + eps)


def _shift_rows(x, s, prev):
    rolled = pltpu.roll(x, s, axis=0)
    prev_rolled = pltpu.roll(prev, s, axis=0)
    row = lax.broadcasted_iota(jnp.int32, prev.shape, 0)
    top = jnp.where(row < s, prev_rolled, rolled[:SUBLANES])
    return jnp.concatenate([top, rolled[SUBLANES:]], axis=0)


def _pair_iotas(shape):
    row = lax.broadcasted_iota(jnp.int32, shape, 0)
    lane = lax.broadcasted_iota(jnp.int32, shape, 1)
    return row, lane


def _bd_mask():
    row, lane = _pair_iotas((LANES, LANES))
    return (row >> HEAD_SHIFT) == (lane >> HEAD_SHIFT)


def _bd(y, bd):
    return jnp.where(bd, jnp.concatenate([y, y], axis=0), 0.0)


def _head_rows(y):
    _, lane = _pair_iotas(y.shape)
    first = lane < HEAD_DIM
    return jnp.concatenate([jnp.where(first, y, 0.0), jnp.where(first, 0.0, y)], axis=0)


def _inv_unit_lower(lows, bd):
    row, lane = _pair_iotas((CHUNK, LANES))
    col = lane & (HEAD_DIM - 1)
    eye = jnp.where(row == col, 1.0, 0.0)
    first = (row >> 1) == (col >> 1)
    ts = [eye - jnp.where(first, low, 0.0) for low in lows]
    for lg in range(1, HEAD_SHIFT):
        off_diag = ((row >> (lg + 1)) == (col >> (lg + 1))) & ((row >> lg) != (col >> lg))
        xs = [_dot(t, _bd(jnp.where(off_diag, low, 0.0), bd)) for t, low in zip(ts, lows)]
        ts = [t - _dot(x, _bd(t, bd)) for t, x in zip(ts, xs)]
    return ts


def _seg_sum(x, ones_bd):
    return _dot(x, ones_bd)


def _mod_kernel(c_ref, w_ref, b_ref, o_ref):
    o_ref[...] = _dot(_silu(c_ref[...]), w_ref[...]) + b_ref[...]


def _mod(c, w_ada, b_ada):
    bsz = c.shape[0]
    n = w_ada.shape[1]
    tn = 1536
    return pl.pallas_call(
        _mod_kernel,
        out_shape=jax.ShapeDtypeStruct((bsz, n), F32),
        grid=(n // tn,),
        in_specs=[pl.BlockSpec((bsz, D_MODEL), lambda j: (0, 0)),
                  pl.BlockSpec((D_MODEL, tn), lambda j: (0, j)),
                  pl.BlockSpec((1, tn), lambda j: (0, j))],
        out_specs=pl.BlockSpec((bsz, tn), lambda j: (0, j)),
        name="mod",
    )(c, w_ada, b_ada.reshape(1, n))


def _inproj_kernel(x_ref, sc_ref, sh_ref, nw_ref, wg_ref, wr_ref, wt_ref, og_ref, or_ref, ot_ref):
    h = _rms(x_ref[0], NORM_EPS) * nw_ref[...] * (1.0 + sc_ref[0]) + sh_ref[0]
    hb = h.astype(BF16)
    step = 512
    for w_ref, o_ref in ((wg_ref, og_ref), (wr_ref, or_ref), (wt_ref, ot_ref)):
        n = w_ref.shape[1]
        for j in range(0, n, step):
            e = min(j + step, n)
            o_ref[0, :, j:e] = jnp.dot(hb, w_ref[:, j:e], preferred_element_type=F32).astype(BF16)


def _inproj(x, scale1, shift1, norm1_w, wg, wr, wt, tm):
    bsz, seq, _ = x.shape
    vec = pl.BlockSpec((1, 1, D_MODEL), lambda b, t: (b, 0, 0))
    tile = lambda n: pl.BlockSpec((1, tm, n), lambda b, t: (b, t, 0))
    return pl.pallas_call(
        _inproj_kernel,
        out_shape=(jax.ShapeDtypeStruct((bsz, seq, GDN_COLS), BF16),
                   jax.ShapeDtypeStruct((bsz, seq, RWKV_COLS), BF16),
                   jax.ShapeDtypeStruct((bsz, seq, GATE_COLS), BF16)),
        grid=(bsz, seq // tm),
        in_specs=[tile(D_MODEL), vec, vec, _resident(norm1_w), _resident(wg), _resident(wr), _resident(wt)],
        out_specs=(tile(GDN_COLS), tile(RWKV_COLS), tile(GATE_COLS)),
        compiler_params=pltpu.CompilerParams(
            dimension_semantics=("parallel", "parallel"), vmem_limit_bytes=VMEM_LIMIT),
        name="inproj",
    )(x, scale1, shift1, norm1_w, wg, wr, wt)


def _gdn_kernel(p_ref, cw_ref, alog_ref, dtb_ref, onw_ref, ones_ref, cum_ref, o_ref,
                tail_ref, s_ref, q_s, k_s, v_s, b_s, g_s, o_s, dec_s, egc_s, erem_s):
    nb, tb = p_ref.shape[0], p_ref.shape[1]
    nch = tb // CHUNK

    @pl.when(pl.program_id(1) == 0)
    def _():
        tail_ref[...] = jnp.zeros_like(tail_ref)
        s_ref[...] = jnp.zeros_like(s_ref)

    ones_bd = ones_ref[...]
    cw = cw_ref[...]
    rowf, lanef = _pair_iotas((CHUNK, WIDTH))
    colf = lanef & (HEAD_DIM - 1)
    causal_f = rowf >= colf
    upper_f = rowf <= colf
    cum_mat = cum_ref[...]

    e_last = {}
    for n in range(nb):
        xin = p_ref[n, :, 0:3 * WIDTH].astype(F32)
        prev = tail_ref[n]
        conv = xin * cw[GDN_CONV - 1:GDN_CONV]
        for s in range(1, GDN_CONV):
            conv = conv + _shift_rows(xin, s, prev) * cw[GDN_CONV - 1 - s:GDN_CONV - s]
        tail_ref[n] = xin[tb - SUBLANES:]
        qkv = _silu(conv)
        q = qkv[:, 0:WIDTH]
        k = qkv[:, WIDTH:2 * WIDTH]
        q_s[n] = q * lax.rsqrt(_seg_sum(q * q, ones_bd) + 1e-6) * (HEAD_DIM ** -0.5)
        k_s[n] = k * lax.rsqrt(_seg_sum(k * k, ones_bd) + 1e-6)
        v_s[n] = qkv[:, 2 * WIDTH:3 * WIDTH]
        b_s[n] = jax.nn.sigmoid(p_ref[n, :, 4 * WIDTH:5 * WIDTH].astype(F32))
        g_s[n] = -jnp.exp(alog_ref[...]) * _softplus(p_ref[n, :, 5 * WIDTH:6 * WIDTH].astype(F32) + dtb_ref[...])
        for c in range(nch):
            rows = slice(c * CHUNK, (c + 1) * CHUNK)
            g = g_s[n, rows, :]
            gx = jnp.concatenate([g, jnp.where(upper_f, g, 0.0)], axis=0)
            dc = _dot2_left(cum_mat, gx)
            diff = dc[:CHUNK]
            gc = dc[CHUNK:]
            dec_s[n, rows, :] = jnp.where(causal_f, jnp.exp(jnp.where(causal_f, diff, 0.0)), 0.0)
            egc_s[n, rows, :] = jnp.exp(gc)
            g_last = gc[CHUNK - 1:CHUNK]
            erem_s[n, rows, :] = jnp.exp(g_last - gc)
            e_last[n, c] = jnp.exp(g_last)

    bd = _bd_mask()
    row, lane = _pair_iotas((CHUNK, LANES))
    col = lane & (HEAD_DIM - 1)
    causal = row >= col
    strict = row > col

    def blk(ref, n, c, p):
        return ref[n, c * CHUNK:(c + 1) * CHUNK, p * LANES:(p + 1) * LANES]

    probs = [(n, c, p) for n in range(nb) for c in range(nch) for p in range(PAIRS)]
    lows, attns = {}, {}
    for n, c, p in probs:
        k2 = blk(k_s, n, c, p)
        a = _dot_nt(jnp.concatenate([k2 * blk(b_s, n, c, p), blk(q_s, n, c, p)], axis=0), _head_rows(k2))
        dec2 = blk(dec_s, n, c, p)
        lows[n, c, p] = jnp.where(strict, a[:CHUNK] * dec2, 0.0)
        attns[n, c, p] = jnp.where(causal, a[CHUNK:] * dec2, 0.0)
    ts = dict(zip(probs, _inv_unit_lower([lows[i] for i in probs], bd)))
    uws = {}
    for n, c, p in probs:
        b2 = blk(b_s, n, c, p)
        rhs = jnp.concatenate([_bd(blk(v_s, n, c, p) * b2, bd),
                               _bd(blk(k_s, n, c, p) * b2 * blk(egc_s, n, c, p), bd)], axis=1)
        uws[n, c, p] = _dot(ts[n, c, p], rhs)

    chains = [(n, p) for n in range(nb) for p in range(PAIRS)]
    state = {(n, p): s_ref[n, p] for n, p in chains}
    for c in range(nch):
        ws = {(n, p): _dot(jnp.concatenate([uws[n, c, p][:, LANES:],
                                            blk(q_s, n, c, p) * blk(egc_s, n, c, p)], axis=0), state[n, p])
              for n, p in chains}
        v_new = {(n, p): uws[n, c, p][:, :LANES] - ws[n, p][:CHUNK] for n, p in chains}
        o_in = {(n, p): _dot(attns[n, c, p], _bd(v_new[n, p], bd)) for n, p in chains}
        upd = {(n, p): _dot_tn(blk(k_s, n, c, p) * blk(erem_s, n, c, p), v_new[n, p]) for n, p in chains}
        for n, p in chains:
            ln = slice(p * LANES, (p + 1) * LANES)
            state[n, p] = state[n, p] * e_last[n, c][:, ln] + jnp.where(bd, upd[n, p], 0.0)
            o_s[n, c * CHUNK:(c + 1) * CHUNK, ln] = ws[n, p][CHUNK:] + o_in[n, p]
    for n, p in chains:
        s_ref[n, p] = state[n, p]

    for n in range(nb):
        o = o_s[n]
        z = p_ref[n, :, 3 * WIDTH:4 * WIDTH].astype(F32)
        ms = _seg_sum(o * o, ones_bd) * (1.0 / HEAD_DIM)
        o_ref[n] = (o * lax.rsqrt(ms + NORM_EPS) * onw_ref[...] * _silu(z)).astype(BF16)


def _gdn(pg, conv_w, alog_b, dtb_b, onw_b, ones_bd, cum_mat, nb, tb):
    bsz, seq, _ = pg.shape
    return pl.pallas_call(
        _gdn_kernel,
        out_shape=jax.ShapeDtypeStruct((bsz, seq, WIDTH), BF16),
        grid=(bsz // nb, seq // tb),
        in_specs=[pl.BlockSpec((nb, tb, GDN_COLS), lambda b, t: (b, t, 0)),
                  _resident(conv_w), _resident(alog_b), _resident(dtb_b), _resident(onw_b),
                  _resident(ones_bd), _resident(cum_mat)],
        out_specs=pl.BlockSpec((nb, tb, WIDTH), lambda b, t: (b, t, 0)),
        scratch_shapes=[pltpu.VMEM((nb, SUBLANES, 3 * WIDTH), F32),
                        pltpu.VMEM((nb, PAIRS, LANES, LANES), F32)]
                       + [pltpu.VMEM((nb, tb, WIDTH), F32) for _ in range(9)],
        compiler_params=pltpu.CompilerParams(
            dimension_semantics=("parallel", "arbitrary"), vmem_limit_bytes=VMEM_LIMIT),
        name="gdn",
    )(pg, conv_w, alog_b, dtb_b, onw_b, ones_bd, cum_mat)


def _rwkv_kernel(p_ref, mu_ref, w0_ref, a0_ref, kk_ref, ka_ref, rk_ref, lw_ref, lb_ref,
                 wwa_ref, g2_ref, ones_ref, tril_ref, o_ref,
                 tail_ref, s_ref, r_s, k_s, v_s, a_s, b_s, y_s, bonus_s, gate_s, bt_s, kt_s):
    nb, tb = p_ref.shape[0], p_ref.shape[1]
    nch = tb // CHUNK

    @pl.when(pl.program_id(1) == 0)
    def _():
        tail_ref[...] = jnp.zeros_like(tail_ref)
        s_ref[...] = jnp.zeros_like(s_ref)

    ones_bd = ones_ref[...]
    tril = tril_ref[...]
    e_last = {}
    for n in range(nb):
        pb = p_ref[n].astype(F32)
        sh = _shift_rows(pb, 1, tail_ref[n])
        tail_ref[n] = pb[tb - SUBLANES:]
        pb = pb + (sh - pb) * mu_ref[...]
        r = pb[:, 0:WIDTH]
        k = pb[:, WIDTH:2 * WIDTH]
        v = pb[:, 2 * WIDTH:3 * WIDTH]
        lo = pb[:, 3 * WIDTH:3 * WIDTH + LANES]
        lane = lax.broadcasted_iota(jnp.int32, lo.shape, 1)
        lo = jnp.where(lane < DECAY_LORA, jnp.tanh(lo), lo)
        wa = _dot(lo, wwa_ref[...])
        w = -_softplus(-(w0_ref[...] + wa[:, :WIDTH])) - 0.5
        a = jax.nn.sigmoid(a0_ref[...] + wa[:, WIDTH:])
        gate_s[n] = _dot(jax.nn.sigmoid(pb[:, 3 * WIDTH + LANES:]), g2_ref[...])
        kk = k * kk_ref[...]
        kk = kk * lax.rsqrt(_seg_sum(kk * kk, ones_bd) + 1e-6)
        k = k * (1.0 + (a - 1.0) * ka_ref[...])
        bonus_s[n] = _seg_sum(r * k * rk_ref[...], ones_bd) * v
        v_s[n] = v
        lw_all = -jnp.exp(w)
        bv_all = kk * a
        for c in range(nch):
            rows = slice(c * CHUNK, (c + 1) * CHUNK)
            lw = lw_all[rows]
            lc = _dot2_left(tril, lw)
            l_last = lc[CHUNK - 1:CHUNK]
            e_neg = jnp.exp(-lc)
            e_rem = jnp.exp(l_last - lc)
            a_s[n, rows, :] = -kk[rows] * jnp.exp(lc - lw)
            r_s[n, rows, :] = r[rows] * jnp.exp(lc)
            bv = bv_all[rows]
            kv = k[rows]
            bt_s[n, rows, :] = bv * e_neg
            kt_s[n, rows, :] = kv * e_neg
            b_s[n, rows, :] = bv * e_rem
            k_s[n, rows, :] = kv * e_rem
            e_last[n, c] = jnp.exp(l_last)

    bd = _bd_mask()
    row, lane2 = _pair_iotas((CHUNK, LANES))
    col = lane2 & (HEAD_DIM - 1)
    incl = row >= col
    strict = row > col

    def blk(ref, n, c, p):
        return ref[n, c * CHUNK:(c + 1) * CHUNK, p * LANES:(p + 1) * LANES]

    probs = [(n, c, p) for n in range(nb) for c in range(nch) for p in range(PAIRS)]
    lows, a_aks, a_rbs, a_rks = {}, {}, {}, {}
    for i in probs:
        sc = _dot_nt(jnp.concatenate([blk(a_s, *i), blk(r_s, *i)], axis=0),
                     jnp.concatenate([_head_rows(blk(bt_s, *i)), _head_rows(blk(kt_s, *i))], axis=0))
        lows[i] = jnp.where(strict, -sc[:CHUNK, :LANES], 0.0)
        a_aks[i] = jnp.where(strict, sc[:CHUNK, LANES:], 0.0)
        a_rbs[i] = jnp.where(incl, sc[CHUNK:, :LANES], 0.0)
        a_rks[i] = jnp.where(incl, sc[CHUNK:, LANES:], 0.0)
    ts = dict(zip(probs, _inv_unit_lower([lows[i] for i in probs], bd)))
    avs = {i: _dot(jnp.concatenate([a_aks[i], a_rks[i]], axis=0), _bd(blk(v_s, *i), bd)) for i in probs}
    tws = {i: _dot(ts[i], jnp.concatenate([_bd(blk(a_s, *i), bd), _bd(avs[i][:CHUNK], bd)], axis=1)) for i in probs}

    chains = [(n, p) for n in range(nb) for p in range(PAIRS)]
    state = {(n, p): s_ref[n, p] for n, p in chains}
    for c in range(nch):
        m1 = {(n, p): _dot_nt(jnp.concatenate([tws[n, c, p][:, :LANES], blk(r_s, n, c, p)], axis=0), state[n, p])
              for n, p in chains}
        us = {(n, p): m1[n, p][:CHUNK] + tws[n, c, p][:, LANES:] for n, p in chains}
        y_in = {(n, p): _dot(a_rbs[n, c, p], _bd(us[n, p], bd)) for n, p in chains}
        upd = {(n, p): _dot_tn(jnp.concatenate([us[n, p], blk(v_s, n, c, p)], axis=0),
                               jnp.concatenate([blk(b_s, n, c, p), blk(k_s, n, c, p)], axis=0)) for n, p in chains}
        for n, p in chains:
            ln = slice(p * LANES, (p + 1) * LANES)
            state[n, p] = state[n, p] * e_last[n, c][:, ln] + jnp.where(bd, upd[n, p], 0.0)
            y_s[n, c * CHUNK:(c + 1) * CHUNK, ln] = m1[n, p][CHUNK:] + y_in[n, p] + avs[n, c, p][CHUNK:]
    for n, p in chains:
        s_ref[n, p] = state[n, p]

    for n in range(nb):
        y = y_s[n]
        mean = _seg_sum(y, ones_bd) * (1.0 / HEAD_DIM)
        yc = y - mean
        var = _seg_sum(yc * yc, ones_bd) * (1.0 / HEAD_DIM)
        yn = yc * lax.rsqrt(var + LNX_EPS) * lw_ref[...] + lb_ref[...]
        o_ref[n] = ((yn + bonus_s[n]) * gate_s[n]).astype(BF16)


def _rwkv(pr, mu, w0, a0, k_k, k_a, r_k, lnx_w, lnx_b, wwa, g2p, ones_bd, tril, nb, tb):
    bsz, seq, _ = pr.shape
    args = (pr, mu, w0, a0, k_k, k_a, r_k, lnx_w, lnx_b, wwa, g2p, ones_bd, tril)
    return pl.pallas_call(
        _rwkv_kernel,
        out_shape=jax.ShapeDtypeStruct((bsz, seq, WIDTH), BF16),
        grid=(bsz // nb, seq // tb),
        in_specs=[pl.BlockSpec((nb, tb, RWKV_COLS), lambda b, t: (b, t, 0))] + [_resident(a) for a in args[1:]],
        out_specs=pl.BlockSpec((nb, tb, WIDTH), lambda b, t: (b, t, 0)),
        scratch_shapes=[pltpu.VMEM((nb, SUBLANES, RWKV_COLS), F32),
                        pltpu.VMEM((nb, PAIRS, LANES, LANES), F32)]
                       + [pltpu.VMEM((nb, tb, WIDTH), F32) for _ in range(10)],
        compiler_params=pltpu.CompilerParams(
            dimension_semantics=("parallel", "arbitrary"), vmem_limit_bytes=VMEM_LIMIT),
        name="rwkv",
    )(*args)


def _merge_kernel(x_ref, ya_ref, yb_ref, gt_ref, g1_ref, wa_ref, wb_ref, wo_ref, o_ref):
    ya = jnp.dot(ya_ref[0], wa_ref[...], preferred_element_type=F32)
    yb = jnp.dot(yb_ref[0], wb_ref[...], preferred_element_type=F32)
    gt = gt_ref[0].astype(F32)
    merged = jax.nn.sigmoid(gt[:, :D_MODEL]) * ya + jax.nn.sigmoid(gt[:, D_MODEL:]) * yb
    o_ref[0] = x_ref[0] + g1_ref[0] * _dot(merged, wo_ref[...])


def _merge(x, ya, yb, gates, gate1, wa, wb, wo, tm):
    bsz, seq, _ = x.shape
    vec = pl.BlockSpec((1, 1, D_MODEL), lambda b, t: (b, 0, 0))
    tile = lambda n: pl.BlockSpec((1, tm, n), lambda b, t: (b, t, 0))
    return pl.pallas_call(
        _merge_kernel,
        out_shape=jax.ShapeDtypeStruct((bsz, seq, D_MODEL), F32),
        grid=(bsz, seq // tm),
        in_specs=[tile(D_MODEL), tile(WIDTH), tile(WIDTH), tile(GATE_COLS), vec,
                  _resident(wa), _resident(wb), _resident(wo)],
        out_specs=tile(D_MODEL),
        compiler_params=pltpu.CompilerParams(
            dimension_semantics=("parallel", "parallel"), vmem_limit_bytes=VMEM_LIMIT),
        name="merge",
    )(x, ya, yb, gates, gate1, wa, wb, wo)


def _ffn_kernel(x_ref, sc_ref, sh_ref, g2_ref, nw_ref, nf_ref, wg_ref, wu_ref, cw_ref, wo_ref, o_ref,
                tail_ref, act_ref):
    tm = x_ref.shape[1]

    @pl.when(pl.program_id(1) == 0)
    def _():
        tail_ref[...] = jnp.zeros_like(tail_ref)

    x1 = x_ref[0]
    h = _rms(x1, NORM_EPS) * nw_ref[...] * (1.0 + sc_ref[0]) + sh_ref[0]
    hb = h.astype(BF16)
    step = 256
    for j in range(0, D_FF, step):
        cs = slice(j, j + step)
        gate = jnp.dot(hb, wg_ref[:, cs], preferred_element_type=F32)
        up = jnp.dot(hb, wu_ref[:, cs], preferred_element_type=F32)
        prev = tail_ref[:, cs]
        cw = cw_ref[:, cs]
        conv = gate * cw[FFN_CONV - 1:FFN_CONV]
        for s in range(1, FFN_CONV):
            conv = conv + _shift_rows(gate, s, prev) * cw[FFN_CONV - 1 - s:FFN_CONV - s]
        tail_ref[:, cs] = gate[tm - SUBLANES:]
        act_ref[:, cs] = (_silu(conv) * up).astype(BF16)
    x2 = x1 + g2_ref[0] * jnp.dot(act_ref[...], wo_ref[...], preferred_element_type=F32)
    o_ref[0] = _rms(x2, NORM_EPS) * nf_ref[...]


def _ffn(x1, scale2, shift2, gate2, norm2_w, norm_f_w, wg, wu, conv_w, wo, tm):
    bsz, seq, _ = x1.shape
    vec = pl.BlockSpec((1, 1, D_MODEL), lambda b, t: (b, 0, 0))
    tile = pl.BlockSpec((1, tm, D_MODEL), lambda b, t: (b, t, 0))
    return pl.pallas_call(
        _ffn_kernel,
        out_shape=jax.ShapeDtypeStruct((bsz, seq, D_MODEL), F32),
        grid=(bsz, seq // tm),
        in_specs=[tile, vec, vec, vec, _resident(norm2_w), _resident(norm_f_w), _resident(wg), _resident(wu),
                  _resident(conv_w), _resident(wo)],
        out_specs=tile,
        scratch_shapes=[pltpu.VMEM((SUBLANES, D_FF), F32), pltpu.VMEM((tm, D_FF), BF16)],
        compiler_params=pltpu.CompilerParams(
            dimension_semantics=("parallel", "arbitrary"), vmem_limit_bytes=VMEM_LIMIT),
        name="ffn",
    )(x1, scale2, shift2, gate2, norm2_w, norm_f_w, wg, wu, conv_w, wo)


def _rep_heads(p):
    return jnp.repeat(p.reshape(1, HEADS), HEAD_DIM, axis=1)


def _const_mats():
    tril64 = (jnp.arange(CHUNK)[:, None] >= jnp.arange(CHUNK)[None, :]).astype(F32)
    ones64 = jnp.ones((CHUNK, CHUNK), F32)
    cum = jnp.concatenate([jnp.concatenate([tril64, -ones64], axis=1),
                           jnp.concatenate([tril64, jnp.zeros_like(ones64)], axis=1)], axis=0)
    w = jnp.arange(WIDTH)
    ones_bd = ((w[:, None] // HEAD_DIM) == (w[None, :] // HEAD_DIM)).astype(BF16)
    return ones_bd, cum.astype(BF16), tril64.astype(BF16)


def kernel(x, c, w_ada, b_ada, norm1_w, w_in, conv_gdn, a_log, dt_bias, onorm_gdn, w_branch_gdn, mu_rwkv, w0, w2, a0, a2, g2, k_k, k_a, r_k, lnx_w, lnx_b, w_branch_rwkv, w_out, norm2_w, w_ffn_in, conv_ffn, w_ffn_out, norm_f_w):
    bsz, seq, _ = x.shape
    tm = min(512, seq)
    tm_ffn = min(1024, seq)
    tb = min(256, seq)
    nb = 2 if bsz % 2 == 0 else 1
    ones_bd, cum_mat, tril = _const_mats()

    mod = _mod(c, w_ada[0], b_ada[0])
    shift1, scale1, gate1, shift2, scale2, gate2 = [m.reshape(bsz, 1, D_MODEL) for m in jnp.split(mod, 6, axis=-1)]

    wi = w_in[0]
    o_z = 3 * WIDTH
    o_b = o_z + WIDTH
    o_a = o_b + HEADS
    o_r = o_a + HEADS
    o_lo = o_r + 3 * WIDTH
    o_g = o_lo + DECAY_LORA + ICLR_LORA
    o_t = o_g + GATE_LORA
    wg = jnp.concatenate([wi[:, :o_b],
                          jnp.repeat(wi[:, o_b:o_a], HEAD_DIM, axis=1),
                          jnp.repeat(wi[:, o_a:o_r], HEAD_DIM, axis=1)], axis=1).astype(BF16)
    pad = GATE_LORA_PAD - GATE_LORA
    wr = jnp.concatenate([wi[:, o_r:o_t], jnp.zeros((D_MODEL, pad), F32)], axis=1).astype(BF16)
    wt = wi[:, o_t:].astype(BF16)
    pg, pr, gates = _inproj(x, scale1, shift1, norm1_w, wg, wr, wt, tm)

    y_a = _gdn(pg, conv_gdn[0], _rep_heads(a_log[0]), _rep_heads(dt_bias[0]),
               jnp.tile(onorm_gdn, (1, HEADS)), ones_bd, cum_mat, nb, tb)

    mu = jnp.concatenate([mu_rwkv[0], jnp.zeros((pad,), F32)]).reshape(1, RWKV_COLS)
    wwa = jnp.zeros((LANES, 2 * WIDTH), F32)
    wwa = wwa.at[:DECAY_LORA, :WIDTH].set(w2[0]).at[DECAY_LORA:, WIDTH:].set(a2[0]).astype(BF16)
    g2p = jnp.concatenate([g2[0], jnp.zeros((pad, WIDTH), F32)], axis=0).astype(BF16)
    y_b = _rwkv(pr, mu, w0, a0, k_k, k_a, r_k.reshape(1, WIDTH), lnx_w, lnx_b, wwa, g2p, ones_bd, tril, nb, tb)

    x1 = _merge(x, y_a, y_b, gates, gate1, w_branch_gdn[0].astype(BF16), w_branch_rwkv[0].astype(BF16),
                w_out[0].astype(BF16), tm)

    wf = w_ffn_in[0]
    return _ffn(x1, scale2, shift2, gate2, norm2_w, norm_f_w.reshape(1, D_MODEL),
                wf[:, :D_FF].astype(BF16), wf[:, D_FF:].astype(BF16), conv_ffn[0], w_ffn_out[0].astype(BF16), tm_ffn)
```

```python
import jax
import jax.numpy as jnp
from jax import lax
from jax.experimental import pallas as pl
from jax.experimental.pallas import tpu as pltpu

F32 = jnp.float32
BF16 = jnp.bfloat16

D_MODEL = 1024
HEADS = 8
HEAD_DIM = 64
HEAD_SHIFT = 6
WIDTH = HEADS * HEAD_DIM
PAIRS = HEADS // 2
LANES = 128
SUBLANES = 8
CHUNK = 64
GDN_CONV = 4
DECAY_LORA = 64
ICLR_LORA = 64
GATE_LORA = 160
GATE_LORA_PAD = 256
D_FF = 2816
FFN_CONV = 3
NORM_EPS = 1e-6
LNX_EPS = 64e-5

GDN_COLS = 6 * WIDTH
RWKV_COLS = 3 * WIDTH + LANES + GATE_LORA_PAD
GATE_COLS = 2 * D_MODEL

VMEM_LIMIT = 56 * 1024 * 1024


def _dot(a, b):
    return jnp.dot(a.astype(BF16), b.astype(BF16), preferred_element_type=F32)


def _dot_nt(a, b):
    return lax.dot_general(a.astype(BF16), b.astype(BF16), (((1,), (1,)), ((), ())),
                           preferred_element_type=F32)


def _dot_tn(a, b):
    return lax.dot_general(a.astype(BF16), b.astype(BF16), (((0,), (0,)), ((), ())),
                           preferred_element_type=F32)


def _dot2_left(a, b):
    hi = b.astype(BF16)
    lo = (b - hi.astype(F32)).astype(BF16)
    return (jnp.dot(a, hi, preferred_element_type=F32) + jnp.dot(a, lo, preferred_element_type=F32))


def _resident(a):
    return pl.BlockSpec(a.shape, lambda b, t: (0,) * a.ndim, pipeline_mode=pl.Buffered(1))


def _silu(x):
    return x * jax.nn.sigmoid(x)


def _softplus(x):
    return jnp.maximum(x, 0.0) + jnp.log(1.0 + jnp.exp(-jnp.abs(x)))


def _rms(x, eps):
    return x * lax.rsqrt(jnp.mean(x * x, axis=-1, keepdims=True) + eps)


def _shift_rows(x, s, prev):
    rolled = pltpu.roll(x, s, axis=0)
    prev_rolled = pltpu.roll(prev, s, axis=0)
    row = lax.broadcasted_iota(jnp.int32, prev.shape, 0)
    top = jnp.where(row < s, prev_rolled, rolled[:SUBLANES])
    return jnp.concatenate([top, rolled[SUBLANES:]], axis=0)


def _pair_iotas(shape):
    row = lax.broadcasted_iota(jnp.int32, shape, 0)
    lane = lax.broadcasted_iota(jnp.int32, shape, 1)
    return row, lane


def _bd_mask():
    row, lane = _pair_iotas((LANES, LANES))
    return (row >> HEAD_SHIFT) == (lane >> HEAD_SHIFT)


def _bd(y, bd):
    return jnp.where(bd, jnp.concatenate([y, y], axis=0), 0.0)


def _head_rows(y):
    _, lane = _pair_iotas(y.shape)
    first = lane < HEAD_DIM
    return jnp.concatenate([jnp.where(first, y, 0.0), jnp.where(first, 0.0, y)], axis=0)


def _interleave(prologue, solve, nb):
    for _ in prologue(0):
        pass
    for n in range(nb):
        nxt = prologue(n + 1) if n + 1 < nb else iter(())
        solve(n, lambda: next(nxt, None))
        for _ in nxt:
            pass


def _inv_unit_lower(lows, bd, tick=lambda: None):
    row, lane = _pair_iotas((CHUNK, LANES))
    col = lane & (HEAD_DIM - 1)
    eye = jnp.where(row == col, 1.0, 0.0)
    first = (row >> 1) == (col >> 1)
    ts = [eye - jnp.where(first, low, 0.0) for low in lows]
    for lg in range(1, HEAD_SHIFT):
        off_diag = ((row >> (lg + 1)) == (col >> (lg + 1))) & ((row >> lg) != (col >> lg))
        xs = [_dot(t, _bd(jnp.where(off_diag, low, 0.0), bd)) for t, low in zip(ts, lows)]
        tick()
        ts = [t - _dot(x, _bd(t, bd)) for t, x in zip(ts, xs)]
        tick()
    return ts


def _seg_sum(x, ones_bd):
    return _dot(x, ones_bd)


def _mod_kernel(c_ref, w_ref, b_ref, o_ref):
    o_ref[...] = _dot(_silu(c_ref[...]), w_ref[...]) + b_ref[...]


def _mod(c, w_ada, b_ada):
    bsz = c.shape[0]
    n = w_ada.shape[1]
    tn = 1536
    return pl.pallas_call(
        _mod_kernel,
        out_shape=jax.ShapeDtypeStruct((bsz, n), F32),
        grid=(n // tn,),
        in_specs=[pl.BlockSpec((bsz, D_MODEL), lambda j: (0, 0)),
                  pl.BlockSpec((D_MODEL, tn), lambda j: (0, j)),
                  pl.BlockSpec((1, tn), lambda j: (0, j))],
        out_specs=pl.BlockSpec((bsz, tn), lambda j: (0, j)),
        name="mod",
    )(c, w_ada, b_ada.reshape(1, n))


def _inproj_kernel(x_ref, sc_ref, sh_ref, nw_ref, wg_ref, wr_ref, wt_ref, og_ref, or_ref, ot_ref):
    h = _rms(x_ref[0], NORM_EPS) * nw_ref[...] * (1.0 + sc_ref[0]) + sh_ref[0]
    hb = h.astype(BF16)
    step = 512
    for w_ref, o_ref in ((wg_ref, og_ref), (wr_ref, or_ref), (wt_ref, ot_ref)):
        n = w_ref.shape[1]
        for j in range(0, n, step):
            e = min(j + step, n)
            o_ref[0, :, j:e] = jnp.dot(hb, w_ref[:, j:e], preferred_element_type=F32).astype(BF16)


def _inproj(x, scale1, shift1, norm1_w, wg, wr, wt, tm):
    bsz, seq, _ = x.shape
    vec = pl.BlockSpec((1, 1, D_MODEL), lambda b, t: (b, 0, 0))
    tile = lambda n: pl.BlockSpec((1, tm, n), lambda b, t: (b, t, 0))
    return pl.pallas_call(
        _inproj_kernel,
        out_shape=(jax.ShapeDtypeStruct((bsz, seq, GDN_COLS), BF16),
                   jax.ShapeDtypeStruct((bsz, seq, RWKV_COLS), BF16),
                   jax.ShapeDtypeStruct((bsz, seq, GATE_COLS), BF16)),
        grid=(bsz, seq // tm),
        in_specs=[tile(D_MODEL), vec, vec, _resident(norm1_w), _resident(wg), _resident(wr), _resident(wt)],
        out_specs=(tile(GDN_COLS), tile(RWKV_COLS), tile(GATE_COLS)),
        compiler_params=pltpu.CompilerParams(
            dimension_semantics=("parallel", "parallel"), vmem_limit_bytes=VMEM_LIMIT),
        name="inproj",
    )(x, scale1, shift1, norm1_w, wg, wr, wt)


def _gdn_kernel(p_ref, cw_ref, alog_ref, dtb_ref, onw_ref, ones_ref, cum_ref, o_ref,
                tail_ref, s_ref, q_s, k_s, v_s, b_s, g_s, o_s, dec_s, egc_s, erem_s):
    nb, tb = p_ref.shape[0], p_ref.shape[1]
    nch = tb // CHUNK

    @pl.when(pl.program_id(1) == 0)
    def _():
        tail_ref[...] = jnp.zeros_like(tail_ref)
        s_ref[...] = jnp.zeros_like(s_ref)

    ones_bd = ones_ref[...]
    cw = cw_ref[...]
    rowf, lanef = _pair_iotas((CHUNK, WIDTH))
    colf = lanef & (HEAD_DIM - 1)
    causal_f = rowf >= colf
    upper_f = rowf <= colf
    cum_mat = cum_ref[...]

    e_last = {}

    def prologue(n):
        xin = p_ref[n, :, 0:3 * WIDTH].astype(F32)
        prev = tail_ref[n]
        conv = xin * cw[GDN_CONV - 1:GDN_CONV]
        for s in range(1, GDN_CONV):
            conv = conv + _shift_rows(xin, s, prev) * cw[GDN_CONV - 1 - s:GDN_CONV - s]
            yield
        tail_ref[n] = xin[tb - SUBLANES:]
        qkv = _silu(conv)
        q = qkv[:, 0:WIDTH]
        k = qkv[:, WIDTH:2 * WIDTH]
        q_s[n] = q * lax.rsqrt(_seg_sum(q * q, ones_bd) + 1e-6) * (HEAD_DIM ** -0.5)
        yield
        k_s[n] = k * lax.rsqrt(_seg_sum(k * k, ones_bd) + 1e-6)
        v_s[n] = qkv[:, 2 * WIDTH:3 * WIDTH]
        yield
        b_s[n] = jax.nn.sigmoid(p_ref[n, :, 4 * WIDTH:5 * WIDTH].astype(F32))
        yield
        g_s[n] = -jnp.exp(alog_ref[...]) * _softplus(p_ref[n, :, 5 * WIDTH:6 * WIDTH].astype(F32) + dtb_ref[...])
        for c in range(nch):
            yield
            rows = slice(c * CHUNK, (c + 1) * CHUNK)
            g = g_s[n, rows, :]
            gx = jnp.concatenate([g, jnp.where(upper_f, g, 0.0)], axis=0)
            dc = _dot2_left(cum_mat, gx)
            diff = dc[:CHUNK]
            gc = dc[CHUNK:]
            dec_s[n, rows, :] = jnp.where(causal_f, jnp.exp(jnp.where(causal_f, diff, 0.0)), 0.0)
            egc_s[n, rows, :] = jnp.exp(gc)
            g_last = gc[CHUNK - 1:CHUNK]
            erem_s[n, rows, :] = jnp.exp(g_last - gc)
            e_last[n, c] = jnp.exp(g_last)

    bd = _bd_mask()
    row, lane = _pair_iotas((CHUNK, LANES))
    col = lane & (HEAD_DIM - 1)
    causal = row >= col
    strict = row > col

    def blk(ref, n, c, p):
        return ref[n, c * CHUNK:(c + 1) * CHUNK, p * LANES:(p + 1) * LANES]

    attns, uws = {}, {}

    def solve(n, tick):
        probs = [(n, c, p) for c in range(nch) for p in range(PAIRS)]
        lows = {}
        for i in probs:
            k2 = blk(k_s, *i)
            a = _dot_nt(jnp.concatenate([k2 * blk(b_s, *i), blk(q_s, *i)], axis=0), _head_rows(k2))
            dec2 = blk(dec_s, *i)
            lows[i] = jnp.where(strict, a[:CHUNK] * dec2, 0.0)
            attns[i] = jnp.where(causal, a[CHUNK:] * dec2, 0.0)
        tick()
        ts = dict(zip(probs, _inv_unit_lower([lows[i] for i in probs], bd, tick)))
        for i in probs:
            b2 = blk(b_s, *i)
            rhs = jnp.concatenate([_bd(blk(v_s, *i) * b2, bd),
                                   _bd(blk(k_s, *i) * b2 * blk(egc_s, *i), bd)], axis=1)
            uws[i] = _dot(ts[i], rhs)

    _interleave(prologue, solve, nb)

    chains = [(n, p) for n in range(nb) for p in range(PAIRS)]
    state = {(n, p): s_ref[n, p] for n, p in chains}
    for c in range(nch):
        ws = {(n, p): _dot(jnp.concatenate([uws[n, c, p][:, LANES:],
                                            blk(q_s, n, c, p) * blk(egc_s, n, c, p)], axis=0), state[n, p])
              for n, p in chains}
        v_new = {(n, p): uws[n, c, p][:, :LANES] - ws[n, p][:CHUNK] for n, p in chains}
        o_in = {(n, p): _dot(attns[n, c, p], _bd(v_new[n, p], bd)) for n, p in chains}
        upd = {(n, p): _dot_tn(blk(k_s, n, c, p) * blk(erem_s, n, c, p), v_new[n, p]) for n, p in chains}
        for n, p in chains:
            ln = slice(p * LANES, (p + 1) * LANES)
            state[n, p] = state[n, p] * e_last[n, c][:, ln] + jnp.where(bd, upd[n, p], 0.0)
            o_s[n, c * CHUNK:(c + 1) * CHUNK, ln] = ws[n, p][CHUNK:] + o_in[n, p]
    for n, p in chains:
        s_ref[n, p] = state[n, p]

    for n in range(nb):
        o = o_s[n]
        z = p_ref[n, :, 3 * WIDTH:4 * WIDTH].astype(F32)
        ms = _seg_sum(o * o, ones_bd) * (1.0 / HEAD_DIM)
        o_ref[n] = (o * lax.rsqrt(ms + NORM_EPS) * onw_ref[...] * _silu(z)).astype(BF16)


def _gdn(pg, conv_w, alog_b, dtb_b, onw_b, ones_bd, cum_mat, nb, tb):
    bsz, seq, _ = pg.shape
    return pl.pallas_call(
        _gdn_kernel,
        out_shape=jax.ShapeDtypeStruct((bsz, seq, WIDTH), BF16),
        grid=(bsz // nb, seq // tb),
        in_specs=[pl.BlockSpec((nb, tb, GDN_COLS), lambda b, t: (b, t, 0)),
                  _resident(conv_w), _resident(alog_b), _resident(dtb_b), _resident(onw_b),
                  _resident(ones_bd), _resident(cum_mat)],
        out_specs=pl.BlockSpec((nb, tb, WIDTH), lambda b, t: (b, t, 0)),
        scratch_shapes=[pltpu.VMEM((nb, SUBLANES, 3 * WIDTH), F32),
                        pltpu.VMEM((nb, PAIRS, LANES, LANES), F32)]
                       + [pltpu.VMEM((nb, tb, WIDTH), F32) for _ in range(9)],
        compiler_params=pltpu.CompilerParams(
            dimension_semantics=("parallel", "arbitrary"), vmem_limit_bytes=VMEM_LIMIT),
        name="gdn",
    )(pg, conv_w, alog_b, dtb_b, onw_b, ones_bd, cum_mat)


def _rwkv_kernel(p_ref, mu_ref, w0_ref, a0_ref, kk_ref, ka_ref, rk_ref, lw_ref, lb_ref,
                 wwa_ref, g2_ref, ones_ref, tril_ref, o_ref,
                 tail_ref, s_ref, r_s, k_s, v_s, a_s, b_s, y_s, bonus_s, gate_s, bt_s, kt_s):
    nb, tb = p_ref.shape[0], p_ref.shape[1]
    nch = tb // CHUNK

    @pl.when(pl.program_id(1) == 0)
    def _():
        tail_ref[...] = jnp.zeros_like(tail_ref)
        s_ref[...] = jnp.zeros_like(s_ref)

    ones_bd = ones_ref[...]
    tril = tril_ref[...]
    e_last = {}

    def prologue(n):
        pb = p_ref[n].astype(F32)
        sh = _shift_rows(pb, 1, tail_ref[n])
        tail_ref[n] = pb[tb - SUBLANES:]
        pb = pb + (sh - pb) * mu_ref[...]
        yield
        r = pb[:, 0:WIDTH]
        k = pb[:, WIDTH:2 * WIDTH]
        v = pb[:, 2 * WIDTH:3 * WIDTH]
        lo = pb[:, 3 * WIDTH:3 * WIDTH + LANES]
        lane = lax.broadcasted_iota(jnp.int32, lo.shape, 1)
        lo = jnp.where(lane < DECAY_LORA, jnp.tanh(lo), lo)
        wa = _dot(lo, wwa_ref[...])
        w = -_softplus(-(w0_ref[...] + wa[:, :WIDTH])) - 0.5
        a = jax.nn.sigmoid(a0_ref[...] + wa[:, WIDTH:])
        yield
        gate_s[n] = _dot(jax.nn.sigmoid(pb[:, 3 * WIDTH + LANES:]), g2_ref[...])
        yield
        kk = k * kk_ref[...]
        kk = kk * lax.rsqrt(_seg_sum(kk * kk, ones_bd) + 1e-6)
        yield
        k = k * (1.0 + (a - 1.0) * ka_ref[...])
        bonus_s[n] = _seg_sum(r * k * rk_ref[...], ones_bd) * v
        v_s[n] = v
        yield
        lw_all = -jnp.exp(w)
        bv_all = kk * a
        for c in range(nch):
            yield
            rows = slice(c * CHUNK, (c + 1) * CHUNK)
            lw = lw_all[rows]
            lc = _dot2_left(tril, lw)
            l_last = lc[CHUNK - 1:CHUNK]
            e_neg = jnp.exp(-lc)
            e_rem = jnp.exp(l_last - lc)
            a_s[n, rows, :] = -kk[rows] * jnp.exp(lc - lw)
            r_s[n, rows, :] = r[rows] * jnp.exp(lc)
            bv = bv_all[rows]
            kv = k[rows]
            bt_s[n, rows, :] = bv * e_neg
            kt_s[n, rows, :] = kv * e_neg
            b_s[n, rows, :] = bv * e_rem
            k_s[n, rows, :] = kv * e_rem
            e_last[n, c] = jnp.exp(l_last)

    bd = _bd_mask()
    row, lane2 = _pair_iotas((CHUNK, LANES))
    col = lane2 & (HEAD_DIM - 1)
    incl = row >= col
    strict = row > col

    def blk(ref, n, c, p):
        return ref[n, c * CHUNK:(c + 1) * CHUNK, p * LANES:(p + 1) * LANES]

    a_rbs, avs, tws = {}, {}, {}

    def solve(n, tick):
        probs = [(n, c, p) for c in range(nch) for p in range(PAIRS)]
        lows, a_aks, a_rks = {}, {}, {}
        for i in probs:
            sc = _dot_nt(jnp.concatenate([blk(a_s, *i), blk(r_s, *i)], axis=0),
                         jnp.concatenate([_head_rows(blk(bt_s, *i)), _head_rows(blk(kt_s, *i))], axis=0))
            lows[i] = jnp.where(strict, -sc[:CHUNK, :LANES], 0.0)
            a_aks[i] = jnp.where(strict, sc[:CHUNK, LANES:], 0.0)
            a_rbs[i] = jnp.where(incl, sc[CHUNK:, :LANES], 0.0)
            a_rks[i] = jnp.where(incl, sc[CHUNK:, LANES:], 0.0)
        tick()
        ts = dict(zip(probs, _inv_unit_lower([lows[i] for i in probs], bd, tick)))
        for i in probs:
            avs[i] = _dot(jnp.concatenate([a_aks[i], a_rks[i]], axis=0), _bd(blk(v_s, *i), bd))
        tick()
        for i in probs:
            tws[i] = _dot(ts[i], jnp.concatenate([_bd(blk(a_s, *i), bd), _bd(avs[i][:CHUNK], bd)], axis=1))

    _interleave(prologue, solve, nb)

    chains = [(n, p) for n in range(nb) for p in range(PAIRS)]
    state = {(n, p): s_ref[n, p] for n, p in chains}
    for c in range(nch):
        m1 = {(n, p): _dot_nt(jnp.concatenate([tws[n, c, p][:, :LANES], blk(r_s, n, c, p)], axis=0), state[n, p])
              for n, p in chains}
        us = {(n, p): m1[n, p][:CHUNK] + tws[n, c, p][:, LANES:] for n, p in chains}
        y_in = {(n, p): _dot(a_rbs[n, c, p], _bd(us[n, p], bd)) for n, p in chains}
        upd = {(n, p): _dot_tn(jnp.concatenate([us[n, p], blk(v_s, n, c, p)], axis=0),
                               jnp.concatenate([blk(b_s, n, c, p), blk(k_s, n, c, p)], axis=0)) for n, p in chains}
        for n, p in chains:
            ln = slice(p * LANES, (p + 1) * LANES)
            state[n, p] = state[n, p] * e_last[n, c][:, ln] + jnp.where(bd, upd[n, p], 0.0)
            y_s[n, c * CHUNK:(c + 1) * CHUNK, ln] = m1[n, p][CHUNK:] + y_in[n, p] + avs[n, c, p][CHUNK:]
    for n, p in chains:
        s_ref[n, p] = state[n, p]

    for n in range(nb):
        y = y_s[n]
        mean = _seg_sum(y, ones_bd) * (1.0 / HEAD_DIM)
        yc = y - mean
        var = _seg_sum(yc * yc, ones_bd) * (1.0 / HEAD_DIM)
        yn = yc * lax.rsqrt(var + LNX_EPS) * lw_ref[...] + lb_ref[...]
        o_ref[n] = ((yn + bonus_s[n]) * gate_s[n]).astype(BF16)


def _rwkv(pr, mu, w0, a0, k_k, k_a, r_k, lnx_w, lnx_b, wwa, g2p, ones_bd, tril, nb, tb):
    bsz, seq, _ = pr.shape
    args = (pr, mu, w0, a0, k_k, k_a, r_k, lnx_w, lnx_b, wwa, g2p, ones_bd, tril)
    return pl.pallas_call(
        _rwkv_kernel,
        out_shape=jax.ShapeDtypeStruct((bsz, seq, WIDTH), BF16),
        grid=(bsz // nb, seq // tb),
        in_specs=[pl.BlockSpec((nb, tb, RWKV_COLS), lambda b, t: (b, t, 0))] + [_resident(a) for a in args[1:]],
        out_specs=pl.BlockSpec((nb, tb, WIDTH), lambda b, t: (b, t, 0)),
        scratch_shapes=[pltpu.VMEM((nb, SUBLANES, RWKV_COLS), F32),
                        pltpu.VMEM((nb, PAIRS, LANES, LANES), F32)]
                       + [pltpu.VMEM((nb, tb, WIDTH), F32) for _ in range(10)],
        compiler_params=pltpu.CompilerParams(
            dimension_semantics=("parallel", "arbitrary"), vmem_limit_bytes=VMEM_LIMIT),
        name="rwkv",
    )(*args)


def _merge_kernel(x_ref, ya_ref, yb_ref, gt_ref, g1_ref, wa_ref, wb_ref, wo_ref, o_ref):
    ya = jnp.dot(ya_ref[0], wa_ref[...], preferred_element_type=F32)
    yb = jnp.dot(yb_ref[0], wb_ref[...], preferred_element_type=F32)
    gt = gt_ref[0].astype(F32)
    merged = jax.nn.sigmoid(gt[:, :D_MODEL]) * ya + jax.nn.sigmoid(gt[:, D_MODEL:]) * yb
    o_ref[0] = x_ref[0] + g1_ref[0] * _dot(merged, wo_ref[...])


def _merge(x, ya, yb, gates, gate1, wa, wb, wo, tm):
    bsz, seq, _ = x.shape
    vec = pl.BlockSpec((1, 1, D_MODEL), lambda b, t: (b, 0, 0))
    tile = lambda n: pl.BlockSpec((1, tm, n), lambda b, t: (b, t, 0))
    return pl.pallas_call(
        _merge_kernel,
        out_shape=jax.ShapeDtypeStruct((bsz, seq, D_MODEL), F32),
        grid=(bsz, seq // tm),
        in_specs=[tile(D_MODEL), tile(WIDTH), tile(WIDTH), tile(GATE_COLS), vec,
                  _resident(wa), _resident(wb), _resident(wo)],
        out_specs=tile(D_MODEL),
        compiler_params=pltpu.CompilerParams(
            dimension_semantics=("parallel", "parallel"), vmem_limit_bytes=VMEM_LIMIT),
        name="merge",
    )(x, ya, yb, gates, gate1, wa, wb, wo)


def _ffn_kernel(x_ref, sc_ref, sh_ref, g2_ref, nw_ref, nf_ref, wg_ref, wu_ref, cw_ref, wo_ref, o_ref,
                tail_ref, act_ref):
    tm = x_ref.shape[1]

    @pl.when(pl.program_id(1) == 0)
    def _():
        tail_ref[...] = jnp.zeros_like(tail_ref)

    x1 = x_ref[0]
    h = _rms(x1, NORM_EPS) * nw_ref[...] * (1.0 + sc_ref[0]) + sh_ref[0]
    hb = h.astype(BF16)
    step = 256
    for j in range(0, D_FF, step):
        cs = slice(j, j + step)
        gate = jnp.dot(hb, wg_ref[:, cs], preferred_element_type=F32)
        up = jnp.dot(hb, wu_ref[:, cs], preferred_element_type=F32)
        prev = tail_ref[:, cs]
        cw = cw_ref[:, cs]
        conv = gate * cw[FFN_CONV - 1:FFN_CONV]
        for s in range(1, FFN_CONV):
            conv = conv + _shift_rows(gate, s, prev) * cw[FFN_CONV - 1 - s:FFN_CONV - s]
        tail_ref[:, cs] = gate[tm - SUBLANES:]
        act_ref[:, cs] = (_silu(conv) * up).astype(BF16)
    x2 = x1 + g2_ref[0] * jnp.dot(act_ref[...], wo_ref[...], preferred_element_type=F32)
    o_ref[0] = _rms(x2, NORM_EPS) * nf_ref[...]


def _ffn(x1, scale2, shift2, gate2, norm2_w, norm_f_w, wg, wu, conv_w, wo, tm):
    bsz, seq, _ = x1.shape
    vec = pl.BlockSpec((1, 1, D_MODEL), lambda b, t: (b, 0, 0))
    tile = pl.BlockSpec((1, tm, D_MODEL), lambda b, t: (b, t, 0))
    return pl.pallas_call(
        _ffn_kernel,
        out_shape=jax.ShapeDtypeStruct((bsz, seq, D_MODEL), F32),
        grid=(bsz, seq // tm),
        in_specs=[tile, vec, vec, vec, _resident(norm2_w), _resident(norm_f_w), _resident(wg), _resident(wu),
                  _resident(conv_w), _resident(wo)],
        out_specs=tile,
        scratch_shapes=[pltpu.VMEM((SUBLANES, D_FF), F32), pltpu.VMEM((tm, D_FF), BF16)],
        compiler_params=pltpu.CompilerParams(
            dimension_semantics=("parallel", "arbitrary"), vmem_limit_bytes=VMEM_LIMIT),
        name="ffn",
    )(x1, scale2, shift2, gate2, norm2_w, norm_f_w, wg, wu, conv_w, wo)


def _rep_heads(p):
    return jnp.repeat(p.reshape(1, HEADS), HEAD_DIM, axis=1)


def _const_mats():
    tril64 = (jnp.arange(CHUNK)[:, None] >= jnp.arange(CHUNK)[None, :]).astype(F32)
    ones64 = jnp.ones((CHUNK, CHUNK), F32)
    cum = jnp.concatenate([jnp.concatenate([tril64, -ones64], axis=1),
                           jnp.concatenate([tril64, jnp.zeros_like(ones64)], axis=1)], axis=0)
    w = jnp.arange(WIDTH)
    ones_bd = ((w[:, None] // HEAD_DIM) == (w[None, :] // HEAD_DIM)).astype(BF16)
    return ones_bd, cum.astype(BF16), tril64.astype(BF16)


def kernel(x, c, w_ada, b_ada, norm1_w, w_in, conv_gdn, a_log, dt_bias, onorm_gdn, w_branch_gdn, mu_rwkv, w0, w2, a0, a2, g2, k_k, k_a, r_k, lnx_w, lnx_b, w_branch_rwkv, w_out, norm2_w, w_ffn_in, conv_ffn, w_ffn_out, norm_f_w):
    bsz, seq, _ = x.shape
    tm = min(512, seq)
    tm_ffn = min(1024, seq)
    tb = min(256, seq)
    nb = next(d for d in (4, 2, 1) if bsz % d == 0)
    ones_bd, cum_mat, tril = _const_mats()

    mod = _mod(c, w_ada[0], b_ada[0])
    shift1, scale1, gate1, shift2, scale2, gate2 = [m.reshape(bsz, 1, D_MODEL) for m in jnp.split(mod, 6, axis=-1)]

    wi = w_in[0]
    o_z = 3 * WIDTH
    o_b = o_z + WIDTH
    o_a = o_b + HEADS
    o_r = o_a + HEADS
    o_lo = o_r + 3 * WIDTH
    o_g = o_lo + DECAY_LORA + ICLR_LORA
    o_t = o_g + GATE_LORA
    wg = jnp.concatenate([wi[:, :o_b],
                          jnp.repeat(wi[:, o_b:o_a], HEAD_DIM, axis=1),
                          jnp.repeat(wi[:, o_a:o_r], HEAD_DIM, axis=1)], axis=1).astype(BF16)
    pad = GATE_LORA_PAD - GATE_LORA
    wr = jnp.concatenate([wi[:, o_r:o_t], jnp.zeros((D_MODEL, pad), F32)], axis=1).astype(BF16)
    wt = wi[:, o_t:].astype(BF16)
    pg, pr, gates = _inproj(x, scale1, shift1, norm1_w, wg, wr, wt, tm)

    y_a = _gdn(pg, conv_gdn[0], _rep_heads(a_log[0]), _rep_heads(dt_bias[0]),
               jnp.tile(onorm_gdn, (1, HEADS)), ones_bd, cum_mat, nb, tb)

    mu = jnp.concatenate([mu_rwkv[0], jnp.zeros((pad,), F32)]).reshape(1, RWKV_COLS)
    wwa = jnp.zeros((LANES, 2 * WIDTH), F32)
    wwa = wwa.at[:DECAY_LORA, :WIDTH].set(w2[0]).at[DECAY_LORA:, WIDTH:].set(a2[0]).astype(BF16)
    g2p = jnp.concatenate([g2[0], jnp.zeros((pad, WIDTH), F32)], axis=0).astype(BF16)
    y_b = _rwkv(pr, mu, w0, a0, k_k, k_a, r_k.reshape(1, WIDTH), lnx_w, lnx_b, wwa, g2p, ones_bd, tril, nb, tb)

    x1 = _merge(x, y_a, y_b, gates, gate1, w_branch_gdn[0].astype(BF16), w_branch_rwkv[0].astype(BF16),
                w_out[0].astype(BF16), tm)

    wf = w_ffn_in[0]
    return _ffn(x1, scale2, shift2, gate2, norm2_w, norm_f_w.reshape(1, D_MODEL),
                wf[:, :D_FF].astype(BF16), wf[:, D_FF:].astype(BF16), conv_ffn[0], w_ffn_out[0].astype(BF16), tm_ffn)
```
